```python
import math
import jax, jax.numpy as jnp
from jax import lax
import numpy as np

D_MODEL = 1024
BATCH = 8
SEQ = 2048
DEPTH = 4

CHUNK = 64
N_MIXERS = 3
EPS = 1e-6
FFN_DIM = 4 * D_MODEL
CONV_KERNEL = 31
SSM_D_INNER = 2 * D_MODEL
SSM_HEAD_DIM = 64
SSM_HEADS = SSM_D_INNER // SSM_HEAD_DIM
SSM_GROUPS = 8
SSM_STATE = 128
SSM_CONV = 4
SSM_CONV_DIM = SSM_D_INNER + 2 * SSM_GROUPS * SSM_STATE
SSM_IN_DIM = SSM_D_INNER + SSM_CONV_DIM + SSM_HEADS
MLA_HEADS = D_MODEL // 64
MLA_NOPE = 64
MLA_ROPE = 32
MLA_V = 64
MLA_Q_LORA = 3 * D_MODEL // 8
MLA_KV_LORA = D_MODEL // 4
MLA_IN_DIM = MLA_Q_LORA + MLA_KV_LORA + MLA_ROPE
ROPE_BASE = 10000.0
Q_BLOCK = 128
N_CONV_LAYERS = (DEPTH + 2) // 3
N_SSM_LAYERS = (DEPTH + 1) // 3
N_MLA_LAYERS = DEPTH // 3

kernel_name = "hybrid_conformer_ssd_mla_trunk"


def rms_norm(x, w):
    xf = x.astype(jnp.float32)
    y = xf * lax.rsqrt(jnp.mean(xf * xf, axis=-1, keepdims=True) + EPS)
    return (y * w.astype(jnp.float32)).astype(x.dtype)


def layer_norm(x, g, b):
    xf = x.astype(jnp.float32)
    mu = jnp.mean(xf, axis=-1, keepdims=True)
    var = jnp.mean(jnp.square(xf - mu), axis=-1, keepdims=True)
    y = (xf - mu) * lax.rsqrt(var + EPS)
    return (y * g.astype(jnp.float32) + b.astype(jnp.float32)).astype(x.dtype)


def causal_dwconv(x, w, b):
    k = w.shape[0]
    xp = jnp.pad(x, ((0, 0), (k - 1, 0), (0, 0)))
    y = lax.conv_general_dilated(xp, w[:, None, :].astype(x.dtype), (1,), 'VALID',
                                 dimension_numbers=('NWC', 'WIO', 'NWC'),
                                 feature_group_count=x.shape[-1])
    return y + b.astype(x.dtype)


def apply_rope(x, positions):
    half = x.shape[-1] // 2
    inv = ROPE_BASE ** (-jnp.arange(half, dtype=jnp.float32) / half)
    ang = positions.astype(jnp.float32)[..., None] * inv
    cos = jnp.cos(ang)[:, :, None, :].astype(x.dtype)
    sin = jnp.sin(ang)[:, :, None, :].astype(x.dtype)
    x1, x2 = x[..., :half], x[..., half:]
    return jnp.concatenate([x1 * cos - x2 * sin, x2 * cos + x1 * sin], axis=-1)


def conformer_conv(h, w_pw1, b_pw1, w_dw, b_dw, ln_g, ln_b, w_pw2, b_pw2):
    u = h @ w_pw1 + b_pw1
    u = u[..., :D_MODEL] * jax.nn.sigmoid(u[..., D_MODEL:])
    u = causal_dwconv(u, w_dw, b_dw)
    u = jax.nn.silu(layer_norm(u, ln_g, ln_b))
    return u @ w_pw2 + b_pw2


def ssd_chunked(xs, dt, a, bm, cm):
    b, s, h, p = xs.shape
    g, n = bm.shape[2], bm.shape[3]
    r = h // g
    c = s // CHUNK
    xdt = (xs * dt[..., None]).reshape(b, c, CHUNK, g, r, p)
    a_dt = (dt * a).reshape(b, c, CHUNK, g, r).transpose(0, 1, 3, 4, 2)
    a_cs = jnp.cumsum(a_dt, axis=-1)
    bc = bm.reshape(b, c, CHUNK, g, n)
    cc = cm.reshape(b, c, CHUNK, g, n)
    tril = jnp.tril(jnp.ones((CHUNK, CHUNK), dtype=bool))
    seg = a_cs[..., :, None] - a_cs[..., None, :]
    decay_in = jnp.exp(jnp.where(tril, seg, -jnp.inf))
    cb = jnp.einsum('bclgn,bcsgn->bcgls', cc, bc)
    y_diag = jnp.einsum('bcgls,bcgrls,bcsgrp->bclgrp', cb, decay_in, xdt)
    decay_to_end = jnp.exp(a_cs[..., -1:] - a_cs)
    states = jnp.einsum('bclgn,bcgrl,bclgrp->bcgrpn', bc, decay_to_end, xdt)
    chunk_decay = jnp.exp(a_cs[..., -1])

    def step(carry, inp):
        st, dec = inp
        return carry * dec[..., None, None] + st, carry

    init = jnp.zeros((b, g, r, p, n), jnp.float32)
    _, prev = lax.scan(step, init, (jnp.moveaxis(states, 1, 0), jnp.moveaxis(chunk_decay, 1, 0)))
    prev = jnp.moveaxis(prev, 0, 1)
    y_off = jnp.einsum('bclgn,bcgrpn,bcgrl->bclgrp', cc, prev, jnp.exp(a_cs))
    return (y_diag + y_off).reshape(b, s, h, p)


def mamba2_ssd(h, w_in, conv_w, conv_b, dt_bias, a_log, d_skip, norm_w, w_out):
    b, s, _ = h.shape
    proj = h @ w_in
    z = proj[..., :SSM_D_INNER]
    xbc = proj[..., SSM_D_INNER:SSM_D_INNER + SSM_CONV_DIM]
    dt = proj[..., SSM_D_INNER + SSM_CONV_DIM:]
    xbc = jax.nn.silu(causal_dwconv(xbc, conv_w, conv_b))
    gn = SSM_GROUPS * SSM_STATE
    xs = xbc[..., :SSM_D_INNER].reshape(b, s, SSM_HEADS, SSM_HEAD_DIM).astype(jnp.float32)
    bm = xbc[..., SSM_D_INNER:SSM_D_INNER + gn].reshape(b, s, SSM_GROUPS, SSM_STATE).astype(jnp.float32)
    cm = xbc[..., SSM_D_INNER + gn:].reshape(b, s, SSM_GROUPS, SSM_STATE).astype(jnp.float32)
    dt = jax.nn.softplus(dt.astype(jnp.float32) + dt_bias.astype(jnp.float32))
    a = -jnp.exp(a_log.astype(jnp.float32))
    y = ssd_chunked(xs, dt, a, bm, cm) + d_skip.astype(jnp.float32)[:, None] * xs
    y = y.reshape(b, s, SSM_D_INNER) * jax.nn.silu(z.astype(jnp.float32))
    y = y.reshape(b, s, SSM_GROUPS, SSM_D_INNER // SSM_GROUPS)
    y = y * lax.rsqrt(jnp.mean(y * y, axis=-1, keepdims=True) + EPS)
    y = y.reshape(b, s, SSM_D_INNER) * norm_w.astype(jnp.float32)
    return y.astype(h.dtype) @ w_out


def mla_attention(h, positions, w_in, q_norm, w_uq, kv_norm, w_ukv, w_o):
    b, s, _ = h.shape
    lat = h @ w_in
    cq = rms_norm(lat[..., :MLA_Q_LORA], q_norm)
    ckv = rms_norm(lat[..., MLA_Q_LORA:MLA_Q_LORA + MLA_KV_LORA], kv_norm)
    k_pe = apply_rope(lat[..., MLA_Q_LORA + MLA_KV_LORA:][:, :, None, :], positions)[:, :, 0, :]
    q = (cq @ w_uq).reshape(b, s, MLA_HEADS, MLA_NOPE + MLA_ROPE)
    q_nope = q[..., :MLA_NOPE]
    q_pe = apply_rope(q[..., MLA_NOPE:], positions)
    kv = (ckv @ w_ukv).reshape(b, s, MLA_HEADS, MLA_NOPE + MLA_V)
    k_nope = kv[..., :MLA_NOPE]
    v = kv[..., MLA_NOPE:]
    nb = s // Q_BLOCK

    def to_blocks(t):
        return jnp.moveaxis(t.reshape(b, nb, Q_BLOCK, MLA_HEADS, t.shape[-1]), 1, 0)

    key_chunk = jnp.arange(s) // CHUNK
    scale = (MLA_NOPE + MLA_ROPE) ** -0.5

    def attend(args):
        qn, qp, blk = args
        q_chunk = (blk * Q_BLOCK + jnp.arange(Q_BLOCK)) // CHUNK
        sc = jnp.einsum('bqhd,bkhd->bhqk', qn, k_nope) + jnp.einsum('bqhd,bkd->bhqk', qp, k_pe)
        sc = sc.astype(jnp.float32) * scale
        sc = jnp.where(key_chunk[None, :] <= q_chunk[:, None], sc, -jnp.inf)
        pr = jax.nn.softmax(sc, axis=-1).astype(v.dtype)
        return jnp.einsum('bhqk,bkhd->bqhd', pr, v)

    out = lax.map(attend, (to_blocks(q_nope), to_blocks(q_pe), jnp.arange(nb)))
    out = jnp.moveaxis(out, 0, 1).reshape(b, s, MLA_HEADS * MLA_V)
    return out @ w_o


def sqrelu_mlp(h, w1, w2):
    return jnp.square(jax.nn.relu(h @ w1)) @ w2


def setup_inputs(seed: int = 0) -> dict:
    key = jax.random.key(seed)
    ks = iter(jax.random.split(key, 40))
    f32 = jnp.float32

    def nrm(shape, scale):
        return jax.random.normal(next(ks), shape, f32) * scale

    def gain(shape):
        return 1.0 + nrm(shape, 0.1)

    nc, ns, nm = N_CONV_LAYERS, N_SSM_LAYERS, N_MLA_LAYERS
    x = nrm((BATCH, SEQ, D_MODEL), 1.0)
    offs = jax.random.randint(next(ks), (BATCH, 1), 0, 16) * CHUNK
    positions = (offs + jnp.arange(SEQ, dtype=jnp.int32)[None, :]).astype(jnp.int32)
    dt0 = jnp.exp(jax.random.uniform(next(ks), (ns, SSM_HEADS), f32, math.log(1e-3), math.log(1e-1)))
    return {
        "x": x,
        "positions": positions,
        "norm_mix_pre": gain((DEPTH, D_MODEL)),
        "norm_mix_post": gain((DEPTH, D_MODEL)),
        "norm_ffn_pre": gain((DEPTH, D_MODEL)),
        "norm_ffn_post": gain((DEPTH, D_MODEL)),
        "ffn_w_in": nrm((DEPTH, D_MODEL, FFN_DIM), D_MODEL ** -0.5),
        "ffn_w_out": nrm((DEPTH, FFN_DIM, D_MODEL), FFN_DIM ** -0.5),
        "conv_w_pw1": nrm((nc, D_MODEL, 2 * D_MODEL), D_MODEL ** -0.5),
        "conv_b_pw1": nrm((nc, 2 * D_MODEL), 0.01),
        "conv_w_dw": nrm((nc, CONV_KERNEL, D_MODEL), CONV_KERNEL ** -0.5),
        "conv_b_dw": nrm((nc, D_MODEL), 0.01),
        "conv_ln_g": gain((nc, D_MODEL)),
        "conv_ln_b": nrm((nc, D_MODEL), 0.01),
        "conv_w_pw2": nrm((nc, D_MODEL, D_MODEL), D_MODEL ** -0.5),
        "conv_b_pw2": nrm((nc, D_MODEL), 0.01),
        "ssm_w_in": nrm((ns, D_MODEL, SSM_IN_DIM), D_MODEL ** -0.5),
        "ssm_conv_w": nrm((ns, SSM_CONV, SSM_CONV_DIM), SSM_CONV ** -0.5),
        "ssm_conv_b": nrm((ns, SSM_CONV_DIM), 0.01),
        "ssm_dt_bias": dt0 + jnp.log(-jnp.expm1(-dt0)),
        "ssm_a_log": jnp.log(jax.random.uniform(next(ks), (ns, SSM_HEADS), f32, 1.0, 16.0)),
        "ssm_d": gain((ns, SSM_HEADS)),
        "ssm_norm_w": gain((ns, SSM_D_INNER)),
        "ssm_w_out": nrm((ns, SSM_D_INNER, D_MODEL), SSM_D_INNER ** -0.5),
        "mla_w_in": nrm((nm, D_MODEL, MLA_IN_DIM), D_MODEL ** -0.5),
        "mla_q_norm": gain((nm, MLA_Q_LORA)),
        "mla_w_uq": nrm((nm, MLA_Q_LORA, MLA_HEADS * (MLA_NOPE + MLA_ROPE)), MLA_Q_LORA ** -0.5),
        "mla_kv_norm": gain((nm, MLA_KV_LORA)),
        "mla_w_ukv": nrm((nm, MLA_KV_LORA, MLA_HEADS * (MLA_NOPE + MLA_V)), MLA_KV_LORA ** -0.5),
        "mla_w_o": nrm((nm, MLA_HEADS * MLA_V, D_MODEL), (MLA_HEADS * MLA_V) ** -0.5),
    }


def reference(x, positions, norm_mix_pre, norm_mix_post, norm_ffn_pre, norm_ffn_post,
              ffn_w_in, ffn_w_out,
              conv_w_pw1, conv_b_pw1, conv_w_dw, conv_b_dw, conv_ln_g, conv_ln_b, conv_w_pw2, conv_b_pw2,
              ssm_w_in, ssm_conv_w, ssm_conv_b, ssm_dt_bias, ssm_a_log, ssm_d, ssm_norm_w, ssm_w_out,
              mla_w_in, mla_q_norm, mla_w_uq, mla_kv_norm, mla_w_ukv, mla_w_o):
    i_conv, i_ssm, i_mla = 0, 0, 0
    for i in range(DEPTH):
        kind = i % N_MIXERS
        h = rms_norm(x, norm_mix_pre[i])
        if kind == 0:
            j = i_conv
            y = conformer_conv(h, conv_w_pw1[j], conv_b_pw1[j], conv_w_dw[j], conv_b_dw[j],
                               conv_ln_g[j], conv_ln_b[j], conv_w_pw2[j], conv_b_pw2[j])
            i_conv += 1
        elif kind == 1:
            j = i_ssm
            y = mamba2_ssd(h, ssm_w_in[j], ssm_conv_w[j], ssm_conv_b[j], ssm_dt_bias[j],
                           ssm_a_log[j], ssm_d[j], ssm_norm_w[j], ssm_w_out[j])
            i_ssm += 1
        else:
            j = i_mla
            y = mla_attention(h, positions, mla_w_in[j], mla_q_norm[j], mla_w_uq[j],
                              mla_kv_norm[j], mla_w_ukv[j], mla_w_o[j])
            i_mla += 1
        x = x + rms_norm(y, norm_mix_post[i])
        h = rms_norm(x, norm_ffn_pre[i])
        x = x + rms_norm(sqrelu_mlp(h, ffn_w_in[i], ffn_w_out[i]), norm_ffn_post[i])
    return x
```

```python
import functools
import math

import jax
import jax.numpy as jnp
from jax import lax
from jax.experimental import pallas as pl
from jax.experimental.pallas import tpu as pltpu

F32 = jnp.float32
BF16 = jnp.bfloat16

D_MODEL = 1024
DEPTH = 4
CHUNK = 64
N_MIXERS = 3
EPS = 1e-6
FFN_DIM = 4 * D_MODEL
CONV_KERNEL = 31
SSM_D_INNER = 2 * D_MODEL
SSM_HEAD_DIM = 64
SSM_HEADS = SSM_D_INNER // SSM_HEAD_DIM
SSM_GROUPS = 8
SSM_STATE = 128
SSM_CONV = 4
SSM_CONV_DIM = SSM_D_INNER + 2 * SSM_GROUPS * SSM_STATE
MLA_HEADS = D_MODEL // 64
MLA_NOPE = 64
MLA_ROPE = 32
MLA_V = 64
MLA_Q_LORA = 3 * D_MODEL // 8
MLA_KV_LORA = D_MODEL // 4
ROPE_BASE = 10000.0

LANES = 128
SUBLANES = 8
VMEM_LIMIT_BYTES = 56 * 1024 * 1024

FFN_TOKENS = 512
FFN_CHUNK = 512
CONV_TOKENS = 512
CONV_HALO = 32
CONV_ROWS = 32
SSD_TOKENS = 256
SSD_CHUNK = 128
SSD_HALO = SUBLANES
MLA_TOKENS = 256
ATT_Q = 256
ATT_PAD = 128
OUT_TOKENS = 512


def _const_spec(shape):
    nd = len(shape)
    return pl.BlockSpec(shape, lambda *_: (0,) * nd, pipeline_mode=pl.Buffered(1))


def _params(n_axes, parallel=True):
    sem = ("parallel" if parallel else "arbitrary",) * n_axes
    return pltpu.CompilerParams(dimension_semantics=sem, vmem_limit_bytes=VMEM_LIMIT_BYTES)


def _rms(x, g):
    return x * lax.rsqrt(jnp.mean(x * x, axis=-1, keepdims=True) + EPS) * g


def _dot(a, b):
    return jnp.dot(a, b, preferred_element_type=F32)


def _dot_nt(a, b):
    return lax.dot_general(a, b, (((1,), (1,)), ((), ())), preferred_element_type=F32)


def _dot_tn(a, b):
    return lax.dot_general(a, b, (((0,), (0,)), ((), ())), preferred_element_type=F32)


def _dot_exact01(m01, x):
    hi = x.astype(BF16)
    r1 = x - hi.astype(F32)
    mid = r1.astype(BF16)
    lo = (r1 - mid.astype(F32)).astype(BF16)
    return _dot(m01, hi) + _dot(m01, mid) + _dot(m01, lo)


def _ffn_body(x_ref, gpre_ref, gpost_ref, w1_ref, w2_ref, o_ref):
    x = x_ref[...]
    h = _rms(x, gpre_ref[...]).astype(BF16)
    acc = None
    for c in range(FFN_DIM // FFN_CHUNK):
        cols = slice(c * FFN_CHUNK, (c + 1) * FFN_CHUNK)
        a = _dot(h, w1_ref[:, cols])
        a = jnp.square(jnp.maximum(a, 0.0)).astype(BF16)
        p = _dot(a, w2_ref[cols, :])
        acc = p if acc is None else acc + p
    o_ref[...] = x + _rms(acc, gpost_ref[...])


def _ffn(x2, gpre, gpost, w1, w2):
    t = x2.shape[0]
    tok = pl.BlockSpec((FFN_TOKENS, D_MODEL), lambda i: (i, 0))
    return pl.pallas_call(
        _ffn_body,
        name="ffn",
        grid=(t // FFN_TOKENS,),
        in_specs=[tok, _const_spec((1, D_MODEL)), _const_spec((1, D_MODEL)),
                  _const_spec((D_MODEL, FFN_DIM)), _const_spec((FFN_DIM, D_MODEL))],
        out_specs=tok,
        out_shape=jax.ShapeDtypeStruct((t, D_MODEL), F32),
        compiler_params=_params(1),
    )(x2, gpre, gpost, w1, w2)


def _conv_body(x_ref, gpre_ref, gpost_ref, w1a_ref, w1b_ref, b1a_ref, b1b_ref, wdw_ref, bdw_ref,
               lng_ref, lnb_ref, w2_ref, b2_ref, o_ref, ubuf, dbuf, vbuf):
    ts = x_ref.shape[0]
    j = pl.program_id(1)

    @pl.when(j == 0)
    def _():
        ubuf[0:CONV_HALO, :] = jnp.zeros((CONV_HALO, D_MODEL), F32)

    @pl.when(j > 0)
    def _():
        ubuf[0:CONV_HALO, :] = ubuf[ts:ts + CONV_HALO, :]

    x = x_ref[...]
    h = _rms(x, gpre_ref[...]).astype(BF16)
    ua = _dot(h, w1a_ref[...]) + b1a_ref[...]
    ub = _dot(h, w1b_ref[...]) + b1b_ref[...]
    ubuf[CONV_HALO:CONV_HALO + ts, :] = ua * jax.nn.sigmoid(ub)

    first = CONV_HALO - (CONV_KERNEL - 1)

    def cols(ci, carry):
        cs = pl.ds(pl.multiple_of(ci * LANES, LANES), LANES)
        for rb in range(ts // CONV_ROWS):
            acc = jnp.broadcast_to(bdw_ref[:, cs], (CONV_ROWS, LANES))
            for k in range(CONV_KERNEL):
                lo = rb * CONV_ROWS + first + k
                acc = acc + wdw_ref[k:k + 1, cs] * ubuf[lo:lo + CONV_ROWS, cs]
            dbuf[rb * CONV_ROWS:(rb + 1) * CONV_ROWS, cs] = acc
        return carry

    lax.fori_loop(0, D_MODEL // LANES, cols, 0)

    def rows(i, carry):
        r0 = pl.multiple_of(i * CONV_ROWS, CONV_ROWS)
        acc = dbuf[pl.ds(r0, CONV_ROWS), :]
        mu = jnp.mean(acc, axis=-1, keepdims=True)
        cen = acc - mu
        var = jnp.mean(cen * cen, axis=-1, keepdims=True)
        y = cen * lax.rsqrt(var + EPS) * lng_ref[...] + lnb_ref[...]
        vbuf[pl.ds(r0, CONV_ROWS), :] = (y * jax.nn.sigmoid(y)).astype(BF16)
        return carry

    lax.fori_loop(0, ts // CONV_ROWS, rows, 0)
    y = _dot(vbuf[...], w2_ref[...]) + b2_ref[...]
    o_ref[...] = x + _rms(y, gpost_ref[...])


def _conv_layer(x3, gpre, gpost, w1a, w1b, b1a, b1b, wdw, bdw, lng, lnb, w2, b2):
    b, s, _ = x3.shape
    ts = CONV_TOKENS
    tok = pl.BlockSpec((None, ts, D_MODEL), lambda bi, j: (bi, j, 0))
    vec = _const_spec((1, D_MODEL))
    mat = _const_spec((D_MODEL, D_MODEL))
    return pl.pallas_call(
        _conv_body,
        name="conv_mixer",
        grid=(b, s // ts),
        in_specs=[tok, vec, vec, mat, mat, vec, vec, _const_spec((CONV_KERNEL, D_MODEL)), vec,
                  vec, vec, mat, vec],
        out_specs=tok,
        out_shape=jax.ShapeDtypeStruct(x3.shape, F32),
        scratch_shapes=[pltpu.VMEM((CONV_HALO + ts, D_MODEL), F32),
                        pltpu.VMEM((ts, D_MODEL), F32),
                        pltpu.VMEM((ts, D_MODEL), BF16)],
        compiler_params=_params(2, parallel=False),
    )(x3, gpre, gpost, w1a, w1b, b1a, b1b, wdw, bdw, lng, lnb, w2, b2)


def _ssd_body(x_ref, gpre_ref, gpost_ref, wz_ref, wxbc_ref, wdt_ref, cw_ref, cb_ref, dtb_ref,
              alog_ref, dskip_ref, nw_ref, wout_ref, o_ref, cbuf, xbuf, dtbuf, zbuf, ybuf, state):
    ts = x_ref.shape[0]
    lc = SSD_CHUNK
    gw = SSM_D_INNER // SSM_GROUPS
    j = pl.program_id(1)

    @pl.when(j == 0)
    def _():
        cbuf[0:SSD_HALO, :] = jnp.zeros((SSD_HALO, SSM_CONV_DIM), F32)
        state[...] = jnp.zeros(state.shape, F32)

    @pl.when(j > 0)
    def _():
        cbuf[0:SSD_HALO, :] = cbuf[ts:ts + SSD_HALO, :]

    x = x_ref[...]
    h = _rms(x, gpre_ref[...]).astype(BF16)
    zbuf[...] = _dot(h, wz_ref[...])
    cbuf[SSD_HALO:SSD_HALO + ts, :] = _dot(h, wxbc_ref[...])
    dt_raw = _dot(h, wdt_ref[...]) + dtb_ref[...]
    dtbuf[...] = jnp.maximum(dt_raw, 0.0) + jnp.log1p(jnp.exp(-jnp.abs(dt_raw)))

    first = SSD_HALO - (SSM_CONV - 1)
    conv = jnp.broadcast_to(cb_ref[...], (ts, SSM_CONV_DIM))
    for k in range(SSM_CONV):
        conv = conv + cw_ref[k:k + 1, :] * cbuf[first + k:first + k + ts, :]
    xbuf[...] = conv * jax.nn.sigmoid(conv)

    a_row = -jnp.exp(alog_ref[...])
    ri = lax.broadcasted_iota(jnp.int32, (lc, lc), 0)
    ci = lax.broadcasted_iota(jnp.int32, (lc, lc), 1)
    causal = ci <= ri
    tril01 = causal.astype(BF16)
    off_b = SSM_D_INNER
    off_c = SSM_D_INNER + SSM_GROUPS * SSM_STATE

    def chunk(c, carry):
        r0 = pl.multiple_of(c * lc, lc)
        rows = pl.ds(r0, lc)
        dt = dtbuf[rows, :]
        a_cs = _dot_exact01(tril01, dt * a_row)
        a_last = a_cs[lc - 1:lc, :]
        exp_a = jnp.exp(a_cs)
        to_end = jnp.exp(a_last - a_cs)
        decay_chunk = jnp.exp(a_last)
        a_cs_t = a_cs.T
        for g in range(SSM_GROUPS):
            gl = slice(g * gw, (g + 1) * gw)
            xs = xbuf[rows, gl]
            dtg = dt[:, gl]
            xdt = xs * dtg
            bm = xbuf[rows, off_b + g * SSM_STATE:off_b + (g + 1) * SSM_STATE].astype(BF16)
            cm = xbuf[rows, off_c + g * SSM_STATE:off_c + (g + 1) * SSM_STATE].astype(BF16)
            cb = _dot_nt(cm, bm)
            lane = lax.broadcasted_iota(jnp.int32, (lc, gw), 1)
            y = None
            for r in range(gw // SSM_HEAD_DIM):
                hc = g * gw + r * SSM_HEAD_DIM
                col = a_cs[:, hc:hc + 1]
                row = a_cs_t[hc:hc + 1, :]
                decay = jnp.where(causal, jnp.exp(col - row), 0.0)
                m = (cb * decay).astype(BF16)
                head = (lane >= r * SSM_HEAD_DIM) & (lane < (r + 1) * SSM_HEAD_DIM)
                xd = jnp.where(head, xdt, 0.0).astype(BF16)
                p = _dot(m, xd)
                y = p if y is None else y + p
            st = state[g]
            y = y + _dot(cm, st.astype(BF16)) * exp_a[:, gl]
            xw = (xdt * to_end[:, gl]).astype(BF16)
            state[g] = st * decay_chunk[:, gl] + _dot_tn(bm, xw)
            y = y + dskip_ref[:, gl] * xs
            zg = zbuf[rows, gl]
            y = y * (zg * jax.nn.sigmoid(zg))
            y = y * lax.rsqrt(jnp.mean(y * y, axis=-1, keepdims=True) + EPS) * nw_ref[:, gl]
            ybuf[rows, gl] = y.astype(BF16)
        return carry

    lax.fori_loop(0, ts // lc, chunk, 0)
    out = _dot(ybuf[...], wout_ref[...])
    o_ref[...] = x + _rms(out, gpost_ref[...])


def _ssd_layer(x3, gpre, gpost, wz, wxbc, wdt, cw, cb, dtb, alog, dskip, nw, wout):
    b, s, _ = x3.shape
    ts = SSD_TOKENS
    tok = pl.BlockSpec((None, ts, D_MODEL), lambda bi, j: (bi, j, 0))
    vec = _const_spec((1, D_MODEL))
    inner = _const_spec((1, SSM_D_INNER))
    return pl.pallas_call(
        _ssd_body,
        name="ssd_mixer",
        grid=(b, s // ts),
        in_specs=[tok, vec, vec, _const_spec((D_MODEL, SSM_D_INNER)),
                  _const_spec((D_MODEL, SSM_CONV_DIM)), _const_spec((D_MODEL, SSM_D_INNER)),
                  _const_spec((SSM_CONV, SSM_CONV_DIM)), _const_spec((1, SSM_CONV_DIM)),
                  inner, inner, inner, inner, _const_spec((SSM_D_INNER, D_MODEL))],
        out_specs=tok,
        out_shape=jax.ShapeDtypeStruct(x3.shape, F32),
        scratch_shapes=[pltpu.VMEM((SSD_HALO + ts, SSM_CONV_DIM), F32),
                        pltpu.VMEM((ts, SSM_CONV_DIM), F32),
                        pltpu.VMEM((ts, SSM_D_INNER), F32),
                        pltpu.VMEM((ts, SSM_D_INNER), F32),
                        pltpu.VMEM((ts, SSM_D_INNER), BF16),
                        pltpu.VMEM((SSM_GROUPS, SSM_STATE, SSM_D_INNER // SSM_GROUPS), F32)],
        compiler_params=_params(2, parallel=False),
    )(x3, gpre, gpost, wz, wxbc, wdt, cw, cb, dtb, alog, dskip, nw, wout)


def _mla_proj_body(x_ref, pos_ref, gpre_ref, wlat_ref, qn_ref, kvn_ref, wq_ref, wqr_ref, wk_ref,
                   wv_ref, inv_ref, q_ref, k_ref, v_ref):
    x = x_ref[...]
    h = _rms(x, gpre_ref[...]).astype(BF16)
    lat = _dot(h, wlat_ref[...])
    o1 = MLA_Q_LORA
    o2 = o1 + MLA_KV_LORA
    cq = _rms(lat[:, :o1], qn_ref[...]).astype(BF16)
    ckv = _rms(lat[:, o1:o2], kvn_ref[...]).astype(BF16)
    pe = lat[:, o2:o2 + ATT_PAD]
    pe_rot = lat[:, o2 + ATT_PAD:o2 + 2 * ATT_PAD]
    ang = pos_ref[...].astype(F32) * inv_ref[...]
    cos = jnp.cos(ang)
    sin = jnp.sin(ang)
    reps = (1, MLA_HEADS)
    k_pe = pe * cos + pe_rot * sin
    k_ref[...] = (_dot(ckv, wk_ref[...]) + jnp.tile(k_pe, reps)).astype(BF16)
    q = _dot(cq, wq_ref[...]) * jnp.tile(cos, reps) + _dot(cq, wqr_ref[...]) * jnp.tile(sin, reps)
    q_ref[...] = q.astype(BF16)
    v_ref[...] = _dot(ckv, wv_ref[...]).astype(BF16)


def _mla_proj(x2, pos2, gpre, wlat, qn, kvn, wq, wqr, wk, wv, inv):
    t = x2.shape[0]
    tm = MLA_TOKENS
    hp = MLA_HEADS * ATT_PAD
    hv = MLA_HEADS * MLA_V

    def tok(w):
        return pl.BlockSpec((tm, w), lambda i: (i, 0))

    return pl.pallas_call(
        _mla_proj_body,
        name="mla_proj",
        grid=(t // tm,),
        in_specs=[tok(D_MODEL), tok(1), _const_spec((1, D_MODEL)), _const_spec(wlat.shape),
                  _const_spec((1, MLA_Q_LORA)), _const_spec((1, MLA_KV_LORA)),
                  _const_spec((MLA_Q_LORA, hp)), _const_spec((MLA_Q_LORA, hp)),
                  _const_spec((MLA_KV_LORA, hp)), _const_spec((MLA_KV_LORA, hv)),
                  _const_spec((1, ATT_PAD))],
        out_specs=[tok(hp), tok(hp), tok(hv)],
        out_shape=[jax.ShapeDtypeStruct((t, hp), BF16), jax.ShapeDtypeStruct((t, hp), BF16),
                   jax.ShapeDtypeStruct((t, hv), BF16)],
        compiler_params=_params(1),
    )(x2, pos2, gpre, wlat, qn, kvn, wq, wqr, wk, wv, inv)


def _attn_body(q_ref, k_ref, v_ref, o_ref):
    tq = q_ref.shape[0]
    i = pl.program_id(2)
    scale = (MLA_NOPE + MLA_ROPE) ** -0.5
    ri = lax.broadcasted_iota(jnp.int32, (tq, tq), 0) // CHUNK
    ci = lax.broadcasted_iota(jnp.int32, (tq, tq), 1) // CHUNK
    diag_mask = ci <= ri

    def step(q, rows, cols, mask, carry):
        m, l, acc = carry
        s = _dot_nt(q, k_ref[rows, cols]) * scale
        if mask is not None:
            s = jnp.where(mask, s, -jnp.inf)
        m_new = jnp.maximum(m, jnp.max(s, axis=-1, keepdims=True))
        alpha = jnp.exp(m - m_new)
        p = jnp.exp(s - m_new)
        l = alpha * l + jnp.sum(p, axis=-1, keepdims=True)
        acc = alpha * acc + _dot(p.astype(BF16), v_ref[rows, :])
        return m_new, l, acc

    outs = []
    for e in range(2):
        cols = slice(e * ATT_PAD, (e + 1) * ATT_PAD)
        q = q_ref[:, cols]
        init = (jnp.full((tq, 1), -jnp.inf, F32), jnp.zeros((tq, 1), F32),
                jnp.zeros((tq, 2 * MLA_V), F32))

        def body(jj, carry, q=q, cols=cols):
            rows = pl.ds(pl.multiple_of(jj * tq, tq), tq)
            return step(q, rows, cols, None, carry)

        carry = lax.fori_loop(0, i, body, init)
        rows = pl.ds(pl.multiple_of(i * tq, tq), tq)
        m, l, acc = step(q, rows, cols, diag_mask, carry)
        outs.append(acc / l)
    lane = lax.broadcasted_iota(jnp.int32, (tq, 2 * MLA_V), 1)
    o_ref[...] = jnp.where(lane < MLA_V, outs[0], outs[1]).astype(BF16)


def _attention(q3, k3, v3):
    b, s, _ = q3.shape
    tq = ATT_Q
    return pl.pallas_call(
        _attn_body,
        name="mla_attn",
        grid=(b, MLA_HEADS // 2, s // tq),
        in_specs=[pl.BlockSpec((None, tq, 2 * ATT_PAD), lambda bi, hp, i: (bi, i, hp)),
                  pl.BlockSpec((None, s, 2 * ATT_PAD), lambda bi, hp, i: (bi, 0, hp)),
                  pl.BlockSpec((None, s, 2 * MLA_V), lambda bi, hp, i: (bi, 0, hp))],
        out_specs=pl.BlockSpec((None, tq, 2 * MLA_V), lambda bi, hp, i: (bi, i, hp)),
        out_shape=jax.ShapeDtypeStruct((b, s, MLA_HEADS * MLA_V), BF16),
        compiler_params=_params(3),
    )(q3, k3, v3)


def _out_proj_body(y_ref, x_ref, w_ref, gpost_ref, o_ref):
    o_ref[...] = x_ref[...] + _rms(_dot(y_ref[...], w_ref[...]), gpost_ref[...])


def _out_proj(y2, x2, w, gpost):
    t, kdim = y2.shape
    tm = OUT_TOKENS
    return pl.pallas_call(
        _out_proj_body,
        name="out_proj",
        grid=(t // tm,),
        in_specs=[pl.BlockSpec((tm, kdim), lambda i: (i, 0)),
                  pl.BlockSpec((tm, D_MODEL), lambda i: (i, 0)),
                  _const_spec((kdim, D_MODEL)), _const_spec((1, D_MODEL))],
        out_specs=pl.BlockSpec((tm, D_MODEL), lambda i: (i, 0)),
        out_shape=jax.ShapeDtypeStruct((t, D_MODEL), F32),
        compiler_params=_params(1),
    )(y2, x2, w, gpost)


def _row(v):
    return v.reshape(1, -1).astype(F32)


def _pad_heads(w, width, offset=0):
    k, hh, d = w.shape
    out = jnp.zeros((k, hh, width), w.dtype)
    out = out.at[:, :, offset:offset + d].set(w)
    return out.reshape(k, hh * width)


def _rot_half_cols(w):
    half = w.shape[-1] // 2
    return jnp.concatenate([-w[..., half:], w[..., :half]], axis=-1)


def _mla_layer(x2, pos2, b, s, gpre, gpost, w_in, q_norm, w_uq, kv_norm, w_ukv, w_o):
    o1 = MLA_Q_LORA
    o2 = o1 + MLA_KV_LORA
    w_pe = w_in[:, o2:]
    slab = jnp.zeros((D_MODEL, ATT_PAD), F32).at[:, MLA_NOPE:MLA_NOPE + MLA_ROPE]
    wlat = jnp.concatenate([w_in[:, :o2], slab.set(w_pe), slab.set(_rot_half_cols(w_pe))],
                           axis=1).astype(BF16)
    wq3 = w_uq.reshape(o1, MLA_HEADS, MLA_NOPE + MLA_ROPE)
    wq = _pad_heads(wq3, ATT_PAD).astype(BF16)
    wqr = _pad_heads(_rot_half_cols(wq3[:, :, MLA_NOPE:]), ATT_PAD, MLA_NOPE).astype(BF16)
    wkv3 = w_ukv.reshape(MLA_KV_LORA, MLA_HEADS, MLA_NOPE + MLA_V)
    wk = _pad_heads(wkv3[:, :, :MLA_NOPE], ATT_PAD).astype(BF16)
    wv = wkv3[:, :, MLA_NOPE:].reshape(MLA_KV_LORA, MLA_HEADS * MLA_V).astype(BF16)
    half = MLA_ROPE // 2
    inv = ROPE_BASE ** (-jnp.arange(half, dtype=F32) / half)
    inv_slab = jnp.zeros((1, ATT_PAD), F32).at[0, MLA_NOPE:MLA_NOPE + MLA_ROPE].set(
        jnp.concatenate([inv, inv]))
    q, k, v = _mla_proj(x2, pos2, _row(gpre), wlat, _row(q_norm), _row(kv_norm), wq, wqr, wk, wv,
                        inv_slab)
    o = _attention(q.reshape(b, s, -1), k.reshape(b, s, -1), v.reshape(b, s, -1))
    return _out_proj(o.reshape(b * s, -1), x2, w_o.astype(BF16), _row(gpost))


def kernel(x, positions, norm_mix_pre, norm_mix_post, norm_ffn_pre, norm_ffn_post, ffn_w_in, ffn_w_out, conv_w_pw1, conv_b_pw1, conv_w_dw, conv_b_dw, conv_ln_g, conv_ln_b, conv_w_pw2, conv_b_pw2, ssm_w_in, ssm_conv_w, ssm_conv_b, ssm_dt_bias, ssm_a_log, ssm_d, ssm_norm_w, ssm_w_out, mla_w_in, mla_q_norm, mla_w_uq, mla_kv_norm, mla_w_ukv, mla_w_o):
    b, s, d = x.shape
    t = b * s
    pos2 = positions.reshape(t, 1)
    i_conv = i_ssm = i_mla = 0
    for i in range(DEPTH):
        kind = i % N_MIXERS
        gpre, gpost = _row(norm_mix_pre[i]), _row(norm_mix_post[i])
        if kind == 0:
            jx = i_conv
            w1 = conv_w_pw1[jx].astype(BF16)
            x = _conv_layer(
                x.reshape(b, s, d), gpre, gpost, w1[:, :D_MODEL], w1[:, D_MODEL:],
                _row(conv_b_pw1[jx, :D_MODEL]), _row(conv_b_pw1[jx, D_MODEL:]),
                conv_w_dw[jx], _row(conv_b_dw[jx]), _row(conv_ln_g[jx]), _row(conv_ln_b[jx]),
                conv_w_pw2[jx].astype(BF16), _row(conv_b_pw2[jx]))
            i_conv += 1
        elif kind == 1:
            jx = i_ssm
            w_in = ssm_w_in[jx]
            o1 = SSM_D_INNER
            o2 = o1 + SSM_CONV_DIM
            rep = SSM_HEAD_DIM
            x = _ssd_layer(
                x.reshape(b, s, d), gpre, gpost, w_in[:, :o1].astype(BF16),
                w_in[:, o1:o2].astype(BF16), jnp.repeat(w_in[:, o2:], rep, axis=1).astype(BF16),
                ssm_conv_w[jx], _row(ssm_conv_b[jx]), _row(jnp.repeat(ssm_dt_bias[jx], rep)),
                _row(jnp.repeat(ssm_a_log[jx], rep)), _row(jnp.repeat(ssm_d[jx], rep)),
                _row(ssm_norm_w[jx]), ssm_w_out[jx].astype(BF16))
            i_ssm += 1
        else:
            jx = i_mla
            x = _mla_layer(x.reshape(t, d), pos2, b, s, norm_mix_pre[i], norm_mix_post[i],
                           mla_w_in[jx], mla_q_norm[jx], mla_w_uq[jx], mla_kv_norm[jx],
                           mla_w_ukv[jx], mla_w_o[jx])
            i_mla += 1
        x = _ffn(x.reshape(t, d), _row(norm_ffn_pre[i]), _row(norm_ffn_post[i]),
                 ffn_w_in[i].astype(BF16), ffn_w_out[i].astype(BF16))
    return x.reshape(b, s, d)
```

```python
import functools
import math

import jax
import jax.numpy as jnp
from jax import lax
from jax.experimental import pallas as pl
from jax.experimental.pallas import tpu as pltpu

F32 = jnp.float32
BF16 = jnp.bfloat16

D_MODEL = 1024
DEPTH = 4
CHUNK = 64
N_MIXERS = 3
EPS = 1e-6
FFN_DIM = 4 * D_MODEL
CONV_KERNEL = 31
SSM_D_INNER = 2 * D_MODEL
SSM_HEAD_DIM = 64
SSM_HEADS = SSM_D_INNER // SSM_HEAD_DIM
SSM_GROUPS = 8
SSM_STATE = 128
SSM_CONV = 4
SSM_CONV_DIM = SSM_D_INNER + 2 * SSM_GROUPS * SSM_STATE
MLA_HEADS = D_MODEL // 64
MLA_NOPE = 64
MLA_ROPE = 32
MLA_V = 64
MLA_Q_LORA = 3 * D_MODEL // 8
MLA_KV_LORA = D_MODEL // 4
ROPE_BASE = 10000.0

LANES = 128
SUBLANES = 8
VMEM_LIMIT_BYTES = 56 * 1024 * 1024

FFN_TOKENS = 512
FFN_CHUNK = 512
CONV_TOKENS = 512
CONV_HALO = 32
CONV_ROWS = 32
CONV_LN_ROWS = 128
SSD_TOKENS = 256
SSD_CHUNK = 128
SSD_HALO = SUBLANES
SSD_CONV_LANES = 512
MLA_TOKENS = 256
ATT_Q = 256
ATT_PAD = 128
OUT_TOKENS = 512


def _const_spec(shape):
    nd = len(shape)
    return pl.BlockSpec(shape, lambda *_: (0,) * nd, pipeline_mode=pl.Buffered(1))


def _params(n_axes, parallel=True):
    sem = ("parallel" if parallel else "arbitrary",) * n_axes
    return pltpu.CompilerParams(dimension_semantics=sem, vmem_limit_bytes=VMEM_LIMIT_BYTES)


def _rms(x, g):
    return x * lax.rsqrt(jnp.mean(x * x, axis=-1, keepdims=True) + EPS) * g


def _dot(a, b):
    return jnp.dot(a, b, preferred_element_type=F32)


def _dot_nt(a, b):
    return lax.dot_general(a, b, (((1,), (1,)), ((), ())), preferred_element_type=F32)


def _dot_tn(a, b):
    return lax.dot_general(a, b, (((0,), (0,)), ((), ())), preferred_element_type=F32)


def _split3(x):
    hi = x.astype(BF16)
    r1 = x - hi.astype(F32)
    mid = r1.astype(BF16)
    lo = (r1 - mid.astype(F32)).astype(BF16)
    return hi, mid, lo


def _dot_exact01(m01, x):
    hi, mid, lo = _split3(x)
    return _dot(m01, hi) + _dot(m01, mid) + _dot(m01, lo)


def _dot_exact01_rhs(x, m01):
    hi, mid, lo = _split3(x)
    return _dot(hi, m01) + _dot(mid, m01) + _dot(lo, m01)


def _ffn_body(x_ref, gpre_ref, gpost_ref, w1_ref, w2_ref, o_ref):
    x = x_ref[...]
    h = _rms(x, gpre_ref[...]).astype(BF16)
    acc = None
    for c in range(FFN_DIM // FFN_CHUNK):
        cols = slice(c * FFN_CHUNK, (c + 1) * FFN_CHUNK)
        a = _dot(h, w1_ref[:, cols])
        a = jnp.square(jnp.maximum(a, 0.0)).astype(BF16)
        p = _dot(a, w2_ref[cols, :])
        acc = p if acc is None else acc + p
    o_ref[...] = x + _rms(acc, gpost_ref[...])


def _ffn(x2, gpre, gpost, w1, w2):
    t = x2.shape[0]
    tok = pl.BlockSpec((FFN_TOKENS, D_MODEL), lambda i: (i, 0))
    return pl.pallas_call(
        _ffn_body,
        name="ffn",
        grid=(t // FFN_TOKENS,),
        in_specs=[tok, _const_spec((1, D_MODEL)), _const_spec((1, D_MODEL)),
                  _const_spec((D_MODEL, FFN_DIM)), _const_spec((FFN_DIM, D_MODEL))],
        out_specs=tok,
        out_shape=jax.ShapeDtypeStruct((t, D_MODEL), F32),
        compiler_params=_params(1),
    )(x2, gpre, gpost, w1, w2)


def _conv_body(x_ref, gpre_ref, gpost_ref, w1a_ref, w1b_ref, b1a_ref, b1b_ref, wdw_ref, bdw_ref,
               lng_ref, lnb_ref, w2_ref, b2_ref, o_ref, ubuf, shifted, dbuf, vbuf):
    ts = x_ref.shape[0]
    j = pl.program_id(1)

    @pl.when(j == 0)
    def _():
        ubuf[0:CONV_HALO, :] = jnp.zeros((CONV_HALO, D_MODEL), F32)

    @pl.when(j > 0)
    def _():
        ubuf[0:CONV_HALO, :] = ubuf[ts:ts + CONV_HALO, :]

    x = x_ref[...]
    h = _rms(x, gpre_ref[...]).astype(BF16)
    ua = _dot(h, w1a_ref[...]) + b1a_ref[...]
    ub = _dot(h, w1b_ref[...]) + b1b_ref[...]
    ubuf[CONV_HALO:CONV_HALO + ts, :] = ua * jax.nn.sigmoid(ub)

    first = CONV_HALO - (CONV_KERNEL - 1)

    n_sh = CONV_HALO + ts - SUBLANES

    def cols(ci, carry):
        cs = pl.ds(pl.multiple_of(ci * LANES, LANES), LANES)
        for sft in range(1, SUBLANES):
            shifted[sft - 1, 0:n_sh, :] = ubuf[sft:sft + n_sh, cs]
        for rb in range(ts // CONV_ROWS):
            acc = jnp.broadcast_to(bdw_ref[:, cs], (CONV_ROWS, LANES))
            for k in range(CONV_KERNEL):
                lo = rb * CONV_ROWS + first + k
                sft = lo % SUBLANES
                if sft:
                    tap = shifted[sft - 1, lo - sft:lo - sft + CONV_ROWS, :]
                else:
                    tap = ubuf[lo:lo + CONV_ROWS, cs]
                acc = acc + wdw_ref[k:k + 1, cs] * tap
            dbuf[rb * CONV_ROWS:(rb + 1) * CONV_ROWS, cs] = acc
        return carry

    lax.fori_loop(0, D_MODEL // LANES, cols, 0)

    def rows(i, carry):
        r0 = pl.multiple_of(i * CONV_LN_ROWS, CONV_LN_ROWS)
        acc = dbuf[pl.ds(r0, CONV_LN_ROWS), :]
        mu = jnp.mean(acc, axis=-1, keepdims=True)
        cen = acc - mu
        var = jnp.mean(cen * cen, axis=-1, keepdims=True)
        y = cen * lax.rsqrt(var + EPS) * lng_ref[...] + lnb_ref[...]
        vbuf[pl.ds(r0, CONV_LN_ROWS), :] = (y * jax.nn.sigmoid(y)).astype(BF16)
        return carry

    lax.fori_loop(0, ts // CONV_LN_ROWS, rows, 0)
    y = _dot(vbuf[...], w2_ref[...]) + b2_ref[...]
    o_ref[...] = x + _rms(y, gpost_ref[...])


def _conv_layer(x3, gpre, gpost, w1a, w1b, b1a, b1b, wdw, bdw, lng, lnb, w2, b2):
    b, s, _ = x3.shape
    ts = CONV_TOKENS
    tok = pl.BlockSpec((None, ts, D_MODEL), lambda bi, j: (bi, j, 0))
    vec = _const_spec((1, D_MODEL))
    mat = _const_spec((D_MODEL, D_MODEL))
    return pl.pallas_call(
        _conv_body,
        name="conv_mixer",
        grid=(b, s // ts),
        in_specs=[tok, vec, vec, mat, mat, vec, vec, _const_spec((CONV_KERNEL, D_MODEL)), vec,
                  vec, vec, mat, vec],
        out_specs=tok,
        out_shape=jax.ShapeDtypeStruct(x3.shape, F32),
        scratch_shapes=[pltpu.VMEM((CONV_HALO + ts, D_MODEL), F32),
                        pltpu.VMEM((SUBLANES - 1, CONV_HALO + ts, LANES), F32),
                        pltpu.VMEM((ts, D_MODEL), F32),
                        pltpu.VMEM((ts, D_MODEL), BF16)],
        compiler_params=_params(2, parallel=False),
    )(x3, gpre, gpost, w1a, w1b, b1a, b1b, wdw, bdw, lng, lnb, w2, b2)


def _ssd_body(x_ref, gpre_ref, gpost_ref, wz_ref, wxbc_ref, wdt_ref, cw_ref, cb_ref, dtb_ref,
              expand_ref, alog_ref, dskip_ref, nw_ref, wout_ref, o_ref, cbuf, xbuf, dtbuf, zbuf,
              ybuf, state):
    ts = x_ref.shape[0]
    lc = SSD_CHUNK
    gw = SSM_D_INNER // SSM_GROUPS
    j = pl.program_id(1)

    @pl.when(j == 0)
    def _():
        cbuf[0:SSD_HALO, :] = jnp.zeros((SSD_HALO, SSM_CONV_DIM), F32)
        state[...] = jnp.zeros(state.shape, F32)

    @pl.when(j > 0)
    def _():
        cbuf[0:SSD_HALO, :] = cbuf[ts:ts + SSD_HALO, :]

    x = x_ref[...]
    h = _rms(x, gpre_ref[...]).astype(BF16)
    zbuf[...] = _dot(h, wz_ref[...])
    cbuf[SSD_HALO:SSD_HALO + ts, :] = _dot(h, wxbc_ref[...])
    dt_raw = _dot(h, wdt_ref[...]) + dtb_ref[...]
    dt_heads = jnp.maximum(dt_raw, 0.0) + jnp.log1p(jnp.exp(-jnp.abs(dt_raw)))
    dtbuf[...] = _dot_exact01_rhs(dt_heads, expand_ref[...])

    first = SSD_HALO - (SSM_CONV - 1)

    def conv_cols(ci, carry):
        cs = pl.ds(pl.multiple_of(ci * SSD_CONV_LANES, SSD_CONV_LANES), SSD_CONV_LANES)
        conv = jnp.broadcast_to(cb_ref[:, cs], (ts, SSD_CONV_LANES))
        for k in range(SSM_CONV):
            conv = conv + cw_ref[k:k + 1, cs] * cbuf[first + k:first + k + ts, cs]
        xbuf[:, cs] = conv * jax.nn.sigmoid(conv)
        return carry

    lax.fori_loop(0, SSM_CONV_DIM // SSD_CONV_LANES, conv_cols, 0)

    a_row = -jnp.exp(alog_ref[...])
    ri = lax.broadcasted_iota(jnp.int32, (lc, lc), 0)
    ci = lax.broadcasted_iota(jnp.int32, (lc, lc), 1)
    causal = ci <= ri
    tril01 = causal.astype(BF16)
    off_b = SSM_D_INNER
    off_c = SSM_D_INNER + SSM_GROUPS * SSM_STATE

    def chunk(c, carry):
        r0 = pl.multiple_of(c * lc, lc)
        rows = pl.ds(r0, lc)
        dt = dtbuf[rows, :]
        a_cs = _dot_exact01(tril01, dt * a_row)
        a_last = a_cs[lc - 1:lc, :]
        exp_a = jnp.exp(a_cs)
        to_end = jnp.exp(a_last - a_cs)
        decay_chunk = jnp.exp(a_last)
        a_cs_t = a_cs.T
        for g in range(SSM_GROUPS):
            gl = slice(g * gw, (g + 1) * gw)
            xs = xbuf[rows, gl]
            dtg = dt[:, gl]
            xdt = xs * dtg
            bm = xbuf[rows, off_b + g * SSM_STATE:off_b + (g + 1) * SSM_STATE].astype(BF16)
            cm = xbuf[rows, off_c + g * SSM_STATE:off_c + (g + 1) * SSM_STATE].astype(BF16)
            cb = _dot_nt(cm, bm)
            lane = lax.broadcasted_iota(jnp.int32, (lc, gw), 1)
            y = None
            for r in range(gw // SSM_HEAD_DIM):
                hc = g * gw + r * SSM_HEAD_DIM
                col = a_cs[:, hc:hc + 1]
                row = a_cs_t[hc:hc + 1, :]
                decay = jnp.where(causal, jnp.exp(col - row), 0.0)
                m = (cb * decay).astype(BF16)
                head = (lane >= r * SSM_HEAD_DIM) & (lane < (r + 1) * SSM_HEAD_DIM)
                xd = jnp.where(head, xdt, 0.0).astype(BF16)
                p = _dot(m, xd)
                y = p if y is None else y + p
            st = state[g]
            y = y + _dot(cm, st.astype(BF16)) * exp_a[:, gl]
            xw = (xdt * to_end[:, gl]).astype(BF16)
            state[g] = st * decay_chunk[:, gl] + _dot_tn(bm, xw)
            y = y + dskip_ref[:, gl] * xs
            zg = zbuf[rows, gl]
            y = y * (zg * jax.nn.sigmoid(zg))
            y = y * lax.rsqrt(jnp.mean(y * y, axis=-1, keepdims=True) + EPS) * nw_ref[:, gl]
            ybuf[rows, gl] = y.astype(BF16)
        return carry

    lax.fori_loop(0, ts // lc, chunk, 0)
    out = _dot(ybuf[...], wout_ref[...])
    o_ref[...] = x + _rms(out, gpost_ref[...])


def _ssd_layer(x3, gpre, gpost, wz, wxbc, wdt, cw, cb, dtb, expand, alog, dskip, nw, wout):
    b, s, _ = x3.shape
    ts = SSD_TOKENS
    tok = pl.BlockSpec((None, ts, D_MODEL), lambda bi, j: (bi, j, 0))
    vec = _const_spec((1, D_MODEL))
    inner = _const_spec((1, SSM_D_INNER))
    return pl.pallas_call(
        _ssd_body,
        name="ssd_mixer",
        grid=(b, s // ts),
        in_specs=[tok, vec, vec, _const_spec((D_MODEL, SSM_D_INNER)),
                  _const_spec((D_MODEL, SSM_CONV_DIM)), _const_spec((D_MODEL, LANES)),
                  _const_spec((SSM_CONV, SSM_CONV_DIM)), _const_spec((1, SSM_CONV_DIM)),
                  _const_spec((1, LANES)), _const_spec((LANES, SSM_D_INNER)),
                  inner, inner, inner, _const_spec((SSM_D_INNER, D_MODEL))],
        out_specs=tok,
        out_shape=jax.ShapeDtypeStruct(x3.shape, F32),
        scratch_shapes=[pltpu.VMEM((SSD_HALO + ts, SSM_CONV_DIM), F32),
                        pltpu.VMEM((ts, SSM_CONV_DIM), F32),
                        pltpu.VMEM((ts, SSM_D_INNER), F32),
                        pltpu.VMEM((ts, SSM_D_INNER), F32),
                        pltpu.VMEM((ts, SSM_D_INNER), BF16),
                        pltpu.VMEM((SSM_GROUPS, SSM_STATE, SSM_D_INNER // SSM_GROUPS), F32)],
        compiler_params=_params(2, parallel=False),
    )(x3, gpre, gpost, wz, wxbc, wdt, cw, cb, dtb, expand, alog, dskip, nw, wout)


def _mla_proj_body(x_ref, pos_ref, gpre_ref, wlat_ref, qn_ref, kvn_ref, wq_ref, wqr_ref, wk_ref,
                   wv_ref, inv_ref, q_ref, k_ref, v_ref):
    x = x_ref[...]
    h = _rms(x, gpre_ref[...]).astype(BF16)
    lat = _dot(h, wlat_ref[...])
    o1 = MLA_Q_LORA
    o2 = o1 + MLA_KV_LORA
    cq = _rms(lat[:, :o1], qn_ref[...]).astype(BF16)
    ckv = _rms(lat[:, o1:o2], kvn_ref[...]).astype(BF16)
    pe = lat[:, o2:o2 + ATT_PAD]
    pe_rot = lat[:, o2 + ATT_PAD:o2 + 2 * ATT_PAD]
    ang = pos_ref[...].astype(F32) * inv_ref[...]
    cos = jnp.cos(ang)
    sin = jnp.sin(ang)
    reps = (1, MLA_HEADS)
    k_pe = pe * cos + pe_rot * sin
    k_ref[...] = (_dot(ckv, wk_ref[...]) + jnp.tile(k_pe, reps)).astype(BF16)
    q = _dot(cq, wq_ref[...]) * jnp.tile(cos, reps) + _dot(cq, wqr_ref[...]) * jnp.tile(sin, reps)
    q_ref[...] = q.astype(BF16)
    v_ref[...] = _dot(ckv, wv_ref[...]).astype(BF16)


def _mla_proj(x2, pos2, gpre, wlat, qn, kvn, wq, wqr, wk, wv, inv):
    t = x2.shape[0]
    tm = MLA_TOKENS
    hp = MLA_HEADS * ATT_PAD
    hv = MLA_HEADS * MLA_V

    def tok(w):
        return pl.BlockSpec((tm, w), lambda i: (i, 0))

    return pl.pallas_call(
        _mla_proj_body,
        name="mla_proj",
        grid=(t // tm,),
        in_specs=[tok(D_MODEL), tok(1), _const_spec((1, D_MODEL)), _const_spec(wlat.shape),
                  _const_spec((1, MLA_Q_LORA)), _const_spec((1, MLA_KV_LORA)),
                  _const_spec((MLA_Q_LORA, hp)), _const_spec((MLA_Q_LORA, hp)),
                  _const_spec((MLA_KV_LORA, hp)), _const_spec((MLA_KV_LORA, hv)),
                  _const_spec((1, ATT_PAD))],
        out_specs=[tok(hp), tok(hp), tok(hv)],
        out_shape=[jax.ShapeDtypeStruct((t, hp), BF16), jax.ShapeDtypeStruct((t, hp), BF16),
                   jax.ShapeDtypeStruct((t, hv), BF16)],
        compiler_params=_params(1),
    )(x2, pos2, gpre, wlat, qn, kvn, wq, wqr, wk, wv, inv)


def _attn_body(q_ref, k_ref, v_ref, o_ref, s_buf):
    s_len = q_ref.shape[0]
    tq = ATT_Q
    c2 = (MLA_NOPE + MLA_ROPE) ** -0.5 * math.log2(math.e)
    ri = lax.broadcasted_iota(jnp.int32, (tq, tq), 0) // CHUNK
    ci = lax.broadcasted_iota(jnp.int32, (tq, tq), 1) // CHUNK
    diag_mask = ci <= ri
    lane = lax.broadcasted_iota(jnp.int32, (tq, 2 * MLA_V), 1)
    for i in range(s_len // tq):
        lo = i * tq
        qrows = slice(lo, lo + tq)
        outs = []
        for e in range(2):
            cols = slice(e * ATT_PAD, (e + 1) * ATT_PAD)
            sb = s_buf.at[2 * (i % 2) + e]
            q = q_ref[qrows, cols]
            sd = jnp.where(diag_mask, _dot_nt(q, k_ref[qrows, cols]), -jnp.inf)
            m = jnp.max(sd, axis=-1, keepdims=True)
            if lo:
                sb[:, 0:lo] = _dot_nt(q, k_ref[0:lo, cols])
                m = jnp.maximum(m, jnp.max(sb[:, 0:lo], axis=-1, keepdims=True))
            pd = jnp.exp2((sd - m) * c2)
            l = jnp.sum(pd, axis=-1, keepdims=True)
            acc = _dot(pd.astype(BF16), v_ref[qrows, :])
            if lo:
                po = jnp.exp2((sb[:, 0:lo] - m) * c2)
                l = l + jnp.sum(po, axis=-1, keepdims=True)
                acc = acc + _dot(po.astype(BF16), v_ref[0:lo, :])
            outs.append(acc / l)
        o_ref[qrows, :] = jnp.where(lane < MLA_V, outs[0], outs[1]).astype(BF16)


def _attention(q3, k3, v3):
    b, s, _ = q3.shape
    return pl.pallas_call(
        _attn_body,
        name="mla_attn",
        grid=(b, MLA_HEADS // 2),
        in_specs=[pl.BlockSpec((None, s, 2 * ATT_PAD), lambda bi, hp: (bi, 0, hp)),
                  pl.BlockSpec((None, s, 2 * ATT_PAD), lambda bi, hp: (bi, 0, hp)),
                  pl.BlockSpec((None, s, 2 * MLA_V), lambda bi, hp: (bi, 0, hp))],
        out_specs=pl.BlockSpec((None, s, 2 * MLA_V), lambda bi, hp: (bi, 0, hp)),
        out_shape=jax.ShapeDtypeStruct((b, s, MLA_HEADS * MLA_V), BF16),
        scratch_shapes=[pltpu.VMEM((4, ATT_Q, s - ATT_Q), F32)],
        compiler_params=_params(2),
    )(q3, k3, v3)


def _out_proj_body(y_ref, x_ref, w_ref, gpost_ref, o_ref):
    o_ref[...] = x_ref[...] + _rms(_dot(y_ref[...], w_ref[...]), gpost_ref[...])


def _out_proj(y2, x2, w, gpost):
    t, kdim = y2.shape
    tm = OUT_TOKENS
    return pl.pallas_call(
        _out_proj_body,
        name="out_proj",
        grid=(t // tm,),
        in_specs=[pl.BlockSpec((tm, kdim), lambda i: (i, 0)),
                  pl.BlockSpec((tm, D_MODEL), lambda i: (i, 0)),
                  _const_spec((kdim, D_MODEL)), _const_spec((1, D_MODEL))],
        out_specs=pl.BlockSpec((tm, D_MODEL), lambda i: (i, 0)),
        out_shape=jax.ShapeDtypeStruct((t, D_MODEL), F32),
        compiler_params=_params(1),
    )(y2, x2, w, gpost)


def _row(v):
    return v.reshape(1, -1).astype(F32)


def _pad_heads(w, width, offset=0):
    k, hh, d = w.shape
    out = jnp.zeros((k, hh, width), w.dtype)
    out = out.at[:, :, offset:offset + d].set(w)
    return out.reshape(k, hh * width)


def _rot_half_cols(w):
    half = w.shape[-1] // 2
    return jnp.concatenate([-w[..., half:], w[..., :half]], axis=-1)


def _mla_layer(x2, pos2, b, s, gpre, gpost, w_in, q_norm, w_uq, kv_norm, w_ukv, w_o):
    o1 = MLA_Q_LORA
    o2 = o1 + MLA_KV_LORA
    w_pe = w_in[:, o2:]
    slab = jnp.zeros((D_MODEL, ATT_PAD), F32).at[:, MLA_NOPE:MLA_NOPE + MLA_ROPE]
    wlat = jnp.concatenate([w_in[:, :o2], slab.set(w_pe), slab.set(_rot_half_cols(w_pe))],
                           axis=1).astype(BF16)
    wq3 = w_uq.reshape(o1, MLA_HEADS, MLA_NOPE + MLA_ROPE)
    wq = _pad_heads(wq3, ATT_PAD).astype(BF16)
    wqr = _pad_heads(_rot_half_cols(wq3[:, :, MLA_NOPE:]), ATT_PAD, MLA_NOPE).astype(BF16)
    wkv3 = w_ukv.reshape(MLA_KV_LORA, MLA_HEADS, MLA_NOPE + MLA_V)
    wk = _pad_heads(wkv3[:, :, :MLA_NOPE], ATT_PAD).astype(BF16)
    wv = wkv3[:, :, MLA_NOPE:].reshape(MLA_KV_LORA, MLA_HEADS * MLA_V).astype(BF16)
    half = MLA_ROPE // 2
    inv = ROPE_BASE ** (-jnp.arange(half, dtype=F32) / half)
    inv_slab = jnp.zeros((1, ATT_PAD), F32).at[0, MLA_NOPE:MLA_NOPE + MLA_ROPE].set(
        jnp.concatenate([inv, inv]))
    q, k, v = _mla_proj(x2, pos2, _row(gpre), wlat, _row(q_norm), _row(kv_norm), wq, wqr, wk, wv,
                        inv_slab)
    o = _attention(q.reshape(b, s, -1), k.reshape(b, s, -1), v.reshape(b, s, -1))
    return _out_proj(o.reshape(b * s, -1), x2, w_o.astype(BF16), _row(gpost))


def kernel(x, positions, norm_mix_pre, norm_mix_post, norm_ffn_pre, norm_ffn_post, ffn_w_in, ffn_w_out, conv_w_pw1, conv_b_pw1, conv_w_dw, conv_b_dw, conv_ln_g, conv_ln_b, conv_w_pw2, conv_b_pw2, ssm_w_in, ssm_conv_w, ssm_conv_b, ssm_dt_bias, ssm_a_log, ssm_d, ssm_norm_w, ssm_w_out, mla_w_in, mla_q_norm, mla_w_uq, mla_kv_norm, mla_w_ukv, mla_w_o):
    b, s, d = x.shape
    t = b * s
    pos2 = positions.reshape(t, 1)
    i_conv = i_ssm = i_mla = 0
    for i in range(DEPTH):
        kind = i % N_MIXERS
        gpre, gpost = _row(norm_mix_pre[i]), _row(norm_mix_post[i])
        if kind == 0:
            jx = i_conv
            w1 = conv_w_pw1[jx].astype(BF16)
            x = _conv_layer(
                x.reshape(b, s, d), gpre, gpost, w1[:, :D_MODEL], w1[:, D_MODEL:],
                _row(conv_b_pw1[jx, :D_MODEL]), _row(conv_b_pw1[jx, D_MODEL:]),
                conv_w_dw[jx], _row(conv_b_dw[jx]), _row(conv_ln_g[jx]), _row(conv_ln_b[jx]),
                conv_w_pw2[jx].astype(BF16), _row(conv_b_pw2[jx]))
            i_conv += 1
        elif kind == 1:
            jx = i_ssm
            w_in = ssm_w_in[jx]
            o1 = SSM_D_INNER
            o2 = o1 + SSM_CONV_DIM
            rep = SSM_HEAD_DIM
            pad = LANES - SSM_HEADS
            expand = jnp.repeat(jnp.eye(LANES, SSM_HEADS, dtype=BF16), rep, axis=1)
            x = _ssd_layer(
                x.reshape(b, s, d), gpre, gpost, w_in[:, :o1].astype(BF16),
                w_in[:, o1:o2].astype(BF16), jnp.pad(w_in[:, o2:], ((0, 0), (0, pad))).astype(BF16),
                ssm_conv_w[jx], _row(ssm_conv_b[jx]), _row(jnp.pad(ssm_dt_bias[jx], (0, pad))),
                expand, _row(jnp.repeat(ssm_a_log[jx], rep)), _row(jnp.repeat(ssm_d[jx], rep)),
                _row(ssm_norm_w[jx]), ssm_w_out[jx].astype(BF16))
            i_ssm += 1
        else:
            jx = i_mla
            x = _mla_layer(x.reshape(t, d), pos2, b, s, norm_mix_pre[i], norm_mix_post[i],
                           mla_w_in[jx], mla_q_norm[jx], mla_w_uq[jx], mla_kv_norm[jx],
                           mla_w_ukv[jx], mla_w_o[jx])
            i_mla += 1
        x = _ffn(x.reshape(t, d), _row(norm_ffn_pre[i]), _row(norm_ffn_post[i]),
                 ffn_w_in[i].astype(BF16), ffn_w_out[i].astype(BF16))
    return x.reshape(b, s, d)
```

```python
import functools
import math

import jax
import jax.numpy as jnp
from jax import lax
from jax.experimental import pallas as pl
from jax.experimental.pallas import tpu as pltpu

F32 = jnp.float32
BF16 = jnp.bfloat16

D_MODEL = 1024
DEPTH = 4
CHUNK = 64
N_MIXERS = 3
EPS = 1e-6
FFN_DIM = 4 * D_MODEL
CONV_KERNEL = 31
SSM_D_INNER = 2 * D_MODEL
SSM_HEAD_DIM = 64
SSM_HEADS = SSM_D_INNER // SSM_HEAD_DIM
SSM_GROUPS = 8
SSM_STATE = 128
SSM_CONV = 4
SSM_CONV_DIM = SSM_D_INNER + 2 * SSM_GROUPS * SSM_STATE
MLA_HEADS = D_MODEL // 64
MLA_NOPE = 64
MLA_ROPE = 32
MLA_V = 64
MLA_Q_LORA = 3 * D_MODEL // 8
MLA_KV_LORA = D_MODEL // 4
ROPE_BASE = 10000.0

LANES = 128
SUBLANES = 8
VMEM_LIMIT_BYTES = 56 * 1024 * 1024

FFN_TOKENS = 512
FFN_CHUNK = 512
CONV_TOKENS = 512
CONV_HALO = 32
CONV_ROWS = 32
CONV_LN_ROWS = 128
CONV_LANES = 256
SSD_TOKENS = 256
SSD_CHUNK = 128
SSD_HALO = SUBLANES
SSD_CONV_LANES = 512
MLA_TOKENS = 256
ATT_Q = 256
ATT_PAD = 128
OUT_TOKENS = 512


def _const_spec(shape):
    nd = len(shape)
    return pl.BlockSpec(shape, lambda *_: (0,) * nd, pipeline_mode=pl.Buffered(1))


def _params(n_axes, parallel=True):
    sem = ("parallel" if parallel else "arbitrary",) * n_axes
    return pltpu.CompilerParams(dimension_semantics=sem, vmem_limit_bytes=VMEM_LIMIT_BYTES)


def _rms(x, g):
    return x * lax.rsqrt(jnp.mean(x * x, axis=-1, keepdims=True) + EPS) * g


def _dot(a, b):
    return jnp.dot(a, b, preferred_element_type=F32)


def _dot_nt(a, b):
    return lax.dot_general(a, b, (((1,), (1,)), ((), ())), preferred_element_type=F32)


def _dot_tn(a, b):
    return lax.dot_general(a, b, (((0,), (0,)), ((), ())), preferred_element_type=F32)


def _split3(x):
    hi = x.astype(BF16)
    r1 = x - hi.astype(F32)
    mid = r1.astype(BF16)
    lo = (r1 - mid.astype(F32)).astype(BF16)
    return hi, mid, lo


def _dot_exact01(m01, x):
    hi, mid, lo = _split3(x)
    return (_dot(jnp.concatenate([m01, m01], axis=1), jnp.concatenate([hi, mid], axis=0))
            + _dot(m01, lo))


def _dot_exact01_rhs(x, m01):
    hi, mid, lo = _split3(x)
    return (_dot(jnp.concatenate([hi, mid], axis=1), jnp.concatenate([m01, m01], axis=0))
            + _dot(lo, m01))


def _ffn_body(x_ref, gpre_ref, gpost_ref, w1_ref, w2_ref, o_ref):
    x = x_ref[...]
    h = _rms(x, gpre_ref[...]).astype(BF16)
    acc = None
    for c in range(FFN_DIM // FFN_CHUNK):
        cols = slice(c * FFN_CHUNK, (c + 1) * FFN_CHUNK)
        a = _dot(h, w1_ref[:, cols])
        a = jnp.square(jnp.maximum(a, 0.0)).astype(BF16)
        p = _dot(a, w2_ref[cols, :])
        acc = p if acc is None else acc + p
    o_ref[...] = x + _rms(acc, gpost_ref[...])


def _layer_spec(shape, layer):
    nd = len(shape)
    return pl.BlockSpec((None,) + tuple(shape), lambda *_: (layer,) + (0,) * nd,
                        pipeline_mode=pl.Buffered(1))


def _ffn(x2, gpre, gpost, w1_all, w2_all, layer):
    t = x2.shape[0]
    tok = pl.BlockSpec((FFN_TOKENS, D_MODEL), lambda i: (i, 0))
    return pl.pallas_call(
        _ffn_body,
        name="ffn",
        grid=(t // FFN_TOKENS,),
        in_specs=[tok, _const_spec((1, D_MODEL)), _const_spec((1, D_MODEL)),
                  _layer_spec((D_MODEL, FFN_DIM), layer), _layer_spec((FFN_DIM, D_MODEL), layer)],
        out_specs=tok,
        out_shape=jax.ShapeDtypeStruct((t, D_MODEL), F32),
        compiler_params=_params(1),
    )(x2, gpre, gpost, w1_all, w2_all)


def _conv_body(x_ref, gpre_ref, gpost_ref, w1_ref, b1_ref, wdw_ref, bdw_ref,
               lng_ref, lnb_ref, w2_ref, b2_ref, o_ref, ubuf, shifted, dbuf):
    ts = x_ref.shape[0]
    j = pl.program_id(1)

    @pl.when(j == 0)
    def _():
        ubuf[0:CONV_HALO, :] = jnp.zeros((CONV_HALO, D_MODEL), F32)

    @pl.when(j > 0)
    def _():
        ubuf[0:CONV_HALO, :] = ubuf[ts:ts + CONV_HALO, :]

    x = x_ref[...]
    h = _rms(x, gpre_ref[...]).astype(BF16)
    first = CONV_HALO - (CONV_KERNEL - 1)
    n_sh = CONV_HALO + ts - SUBLANES

    for cblk in range(D_MODEL // CONV_LANES):
        cs = slice(cblk * CONV_LANES, (cblk + 1) * CONV_LANES)
        gs = slice(D_MODEL + cblk * CONV_LANES, D_MODEL + (cblk + 1) * CONV_LANES)
        ua = _dot(h, w1_ref[:, cs]) + b1_ref[:, cs]
        ub = _dot(h, w1_ref[:, gs]) + b1_ref[:, gs]
        ubuf[CONV_HALO:CONV_HALO + ts, cs] = ua * jax.nn.sigmoid(ub)
        for sft in range(1, SUBLANES):
            shifted[sft - 1, 0:n_sh, cs] = ubuf[sft:sft + n_sh, cs]
        for sub in range(CONV_LANES // LANES):
            ls = slice(cblk * CONV_LANES + sub * LANES, cblk * CONV_LANES + (sub + 1) * LANES)
            for rb in range(ts // CONV_ROWS):
                acc = jnp.broadcast_to(bdw_ref[:, ls], (CONV_ROWS, LANES))
                for k in range(CONV_KERNEL):
                    lo = rb * CONV_ROWS + first + k
                    sft = lo % SUBLANES
                    if sft:
                        tap = shifted[sft - 1, lo - sft:lo - sft + CONV_ROWS, ls]
                    else:
                        tap = ubuf[lo:lo + CONV_ROWS, ls]
                    acc = acc + wdw_ref[k:k + 1, ls] * tap
                dbuf[rb * CONV_ROWS:(rb + 1) * CONV_ROWS, ls] = acc

    for rt in range(ts // CONV_LN_ROWS):
        rows = slice(rt * CONV_LN_ROWS, (rt + 1) * CONV_LN_ROWS)
        acc = dbuf[rows, :]
        mu = jnp.mean(acc, axis=-1, keepdims=True)
        cen = acc - mu
        var = jnp.mean(cen * cen, axis=-1, keepdims=True)
        y = cen * lax.rsqrt(var + EPS) * lng_ref[...] + lnb_ref[...]
        v = (y * jax.nn.sigmoid(y)).astype(BF16)
        out = _dot(v, w2_ref[...]) + b2_ref[...]
        o_ref[rows, :] = x[rows, :] + _rms(out, gpost_ref[...])


def _conv_layer(x3, gpre, gpost, w1_all, b1, wdw, bdw, lng, lnb, w2_all, b2, layer):
    b, s, _ = x3.shape
    ts = CONV_TOKENS
    tok = pl.BlockSpec((None, ts, D_MODEL), lambda bi, j: (bi, j, 0))
    vec = _const_spec((1, D_MODEL))
    return pl.pallas_call(
        _conv_body,
        name="conv_mixer",
        grid=(b, s // ts),
        in_specs=[tok, vec, vec, _layer_spec((D_MODEL, 2 * D_MODEL), layer),
                  _const_spec((1, 2 * D_MODEL)), _const_spec((CONV_KERNEL, D_MODEL)), vec,
                  vec, vec, _layer_spec((D_MODEL, D_MODEL), layer), vec],
        out_specs=tok,
        out_shape=jax.ShapeDtypeStruct(x3.shape, F32),
        scratch_shapes=[pltpu.VMEM((CONV_HALO + ts, D_MODEL), F32),
                        pltpu.VMEM((SUBLANES - 1, CONV_HALO + ts, D_MODEL), F32),
                        pltpu.VMEM((ts, D_MODEL), F32)],
        compiler_params=_params(2, parallel=False),
    )(x3, gpre, gpost, w1_all, b1, wdw, bdw, lng, lnb, w2_all, b2)


def _ssd_body(x_ref, gpre_ref, gpost_ref, wz_ref, wxbc_ref, wdt_ref, cw_ref, cb_ref, dtb_ref,
              expand_ref, alogh_ref, alog_ref, dskip_ref, nw_ref, wout_ref, o_ref, cbuf, xbuf,
              dtbuf, zbuf, ybuf, state):
    ts = x_ref.shape[0]
    lc = SSD_CHUNK
    gw = SSM_D_INNER // SSM_GROUPS
    j = pl.program_id(1)

    @pl.when(j == 0)
    def _():
        cbuf[0:SSD_HALO, :] = jnp.zeros((SSD_HALO, SSM_CONV_DIM), F32)
        state[...] = jnp.zeros(state.shape, F32)

    @pl.when(j > 0)
    def _():
        cbuf[0:SSD_HALO, :] = cbuf[ts:ts + SSD_HALO, :]

    x = x_ref[...]
    h = _rms(x, gpre_ref[...]).astype(BF16)
    dt_raw = _dot(h, wdt_ref[...]) + dtb_ref[...]
    dt_heads = jnp.maximum(dt_raw, 0.0) + jnp.log1p(jnp.exp(-jnp.abs(dt_raw)))
    dtbuf[...] = _dot_exact01_rhs(dt_heads, expand_ref[...])
    zbuf[...] = _dot(h, wz_ref[...])

    first = SSD_HALO - (SSM_CONV - 1)
    for cblk in range(SSM_CONV_DIM // SSD_CONV_LANES):
        cs = slice(cblk * SSD_CONV_LANES, (cblk + 1) * SSD_CONV_LANES)
        cbuf[SSD_HALO:SSD_HALO + ts, cs] = _dot(h, wxbc_ref[:, cs])
        conv = jnp.broadcast_to(cb_ref[:, cs], (ts, SSD_CONV_LANES))
        for k in range(SSM_CONV):
            conv = conv + cw_ref[k:k + 1, cs] * cbuf[first + k:first + k + ts, cs]
        xbuf[:, cs] = conv * jax.nn.sigmoid(conv)

    a_row = -jnp.exp(alog_ref[...])
    a_heads = -jnp.exp(alogh_ref[...])
    ri = lax.broadcasted_iota(jnp.int32, (lc, lc), 0)
    ci = lax.broadcasted_iota(jnp.int32, (lc, lc), 1)
    causal = ci <= ri
    tril01 = causal.astype(BF16)
    off_b = SSM_D_INNER
    off_c = SSM_D_INNER + SSM_GROUPS * SSM_STATE
    lane = lax.broadcasted_iota(jnp.int32, (1, gw), 1)
    head01 = [((lane >= r * SSM_HEAD_DIM) & (lane < (r + 1) * SSM_HEAD_DIM)).astype(BF16)
              for r in range(gw // SSM_HEAD_DIM)]

    for c in range(ts // lc):
        rows = slice(c * lc, (c + 1) * lc)
        dt = dtbuf[rows, :]
        a_cs = _dot_exact01(tril01, dt * a_row)
        a_last = a_cs[lc - 1:lc, :]
        exp_a = jnp.exp(a_cs)
        to_end = jnp.exp(a_last - a_cs)
        decay_chunk = jnp.exp(a_last)
        a_csh = _dot_exact01(tril01, dt_heads[rows, :] * a_heads)
        a_csh_t = a_csh.T
        for g in range(SSM_GROUPS):
            gl = slice(g * gw, (g + 1) * gw)
            xs = xbuf[rows, gl]
            xdt = xs * dt[:, gl]
            xdb = xdt.astype(BF16)
            bm = xbuf[rows, off_b + g * SSM_STATE:off_b + (g + 1) * SSM_STATE].astype(BF16)
            cm = xbuf[rows, off_c + g * SSM_STATE:off_c + (g + 1) * SSM_STATE].astype(BF16)
            cb = _dot_nt(cm, bm)
            ms, xds = [], []
            for r in range(gw // SSM_HEAD_DIM):
                hd = g * (gw // SSM_HEAD_DIM) + r
                col = a_csh[:, hd:hd + 1]
                row = a_csh_t[hd:hd + 1, :]
                decay = jnp.where(causal, jnp.exp(col - row), 0.0)
                ms.append((cb * decay).astype(BF16))
                xds.append(xdb * head01[r])
            y = (_dot(jnp.concatenate(ms[0:2], axis=1), jnp.concatenate(xds[0:2], axis=0))
                 + _dot(jnp.concatenate(ms[2:4], axis=1), jnp.concatenate(xds[2:4], axis=0)))
            st = state[g]
            y = y + _dot(cm, st.astype(BF16)) * exp_a[:, gl]
            xw = (xdt * to_end[:, gl]).astype(BF16)
            state[g] = st * decay_chunk[:, gl] + _dot_tn(bm, xw)
            y = y + dskip_ref[:, gl] * xs
            zg = zbuf[rows, gl]
            y = y * (zg * jax.nn.sigmoid(zg))
            y = y * lax.rsqrt(jnp.mean(y * y, axis=-1, keepdims=True) + EPS) * nw_ref[:, gl]
            ybuf[rows, gl] = y.astype(BF16)
        out = _dot(ybuf[rows, :], wout_ref[...])
        o_ref[rows, :] = x[rows, :] + _rms(out, gpost_ref[...])


def _ssd_layer(x3, gpre, gpost, wz, wxbc, wdt, cw, cb, dtb, expand, alogh, alog, dskip, nw, wout):
    b, s, _ = x3.shape
    ts = SSD_TOKENS
    tok = pl.BlockSpec((None, ts, D_MODEL), lambda bi, j: (bi, j, 0))
    vec = _const_spec((1, D_MODEL))
    inner = _const_spec((1, SSM_D_INNER))
    return pl.pallas_call(
        _ssd_body,
        name="ssd_mixer",
        grid=(b, s // ts),
        in_specs=[tok, vec, vec, _const_spec((D_MODEL, SSM_D_INNER)),
                  _const_spec((D_MODEL, SSM_CONV_DIM)), _const_spec((D_MODEL, LANES)),
                  _const_spec((SSM_CONV, SSM_CONV_DIM)), _const_spec((1, SSM_CONV_DIM)),
                  _const_spec((1, LANES)), _const_spec((LANES, SSM_D_INNER)), _const_spec((1, LANES)),
                  inner, inner, inner, _const_spec((SSM_D_INNER, D_MODEL))],
        out_specs=tok,
        out_shape=jax.ShapeDtypeStruct(x3.shape, F32),
        scratch_shapes=[pltpu.VMEM((SSD_HALO + ts, SSM_CONV_DIM), F32),
                        pltpu.VMEM((ts, SSM_CONV_DIM), F32),
                        pltpu.VMEM((ts, SSM_D_INNER), F32),
                        pltpu.VMEM((ts, SSM_D_INNER), F32),
                        pltpu.VMEM((ts, SSM_D_INNER), BF16),
                        pltpu.VMEM((SSM_GROUPS, SSM_STATE, SSM_D_INNER // SSM_GROUPS), F32)],
        compiler_params=_params(2, parallel=False),
    )(x3, gpre, gpost, wz, wxbc, wdt, cw, cb, dtb, expand, alogh, alog, dskip, nw, wout)


def _mla_proj_body(x_ref, pos_ref, gpre_ref, wlat_ref, qn_ref, kvn_ref, wq_ref, wqr_ref, wk_ref,
                   wv_ref, inv_ref, q_ref, k_ref, v_ref):
    x = x_ref[...]
    h = _rms(x, gpre_ref[...]).astype(BF16)
    lat = _dot(h, wlat_ref[...])
    o1 = MLA_Q_LORA
    o2 = o1 + MLA_KV_LORA
    cq = _rms(lat[:, :o1], qn_ref[...]).astype(BF16)
    ckv = _rms(lat[:, o1:o2], kvn_ref[...]).astype(BF16)
    pe = lat[:, o2:o2 + ATT_PAD]
    pe_rot = lat[:, o2 + ATT_PAD:o2 + 2 * ATT_PAD]
    ang = pos_ref[...].astype(F32) * inv_ref[...]
    cos = jnp.cos(ang)
    sin = jnp.sin(ang)
    reps = (1, MLA_HEADS)
    k_pe = pe * cos + pe_rot * sin
    k_ref[...] = (_dot(ckv, wk_ref[...]) + jnp.tile(k_pe, reps)).astype(BF16)
    q = _dot(cq, wq_ref[...]) * jnp.tile(cos, reps) + _dot(cq, wqr_ref[...]) * jnp.tile(sin, reps)
    q_ref[...] = q.astype(BF16)
    v_ref[...] = _dot(ckv, wv_ref[...]).astype(BF16)


def _mla_proj(x2, pos2, gpre, wlat, qn, kvn, wq, wqr, wk, wv, inv):
    t = x2.shape[0]
    tm = MLA_TOKENS
    hp = MLA_HEADS * ATT_PAD
    hv = MLA_HEADS * MLA_V

    def tok(w):
        return pl.BlockSpec((tm, w), lambda i: (i, 0))

    return pl.pallas_call(
        _mla_proj_body,
        name="mla_proj",
        grid=(t // tm,),
        in_specs=[tok(D_MODEL), tok(1), _const_spec((1, D_MODEL)), _const_spec(wlat.shape),
                  _const_spec((1, MLA_Q_LORA)), _const_spec((1, MLA_KV_LORA)),
                  _const_spec((MLA_Q_LORA, hp)), _const_spec((MLA_Q_LORA, hp)),
                  _const_spec((MLA_KV_LORA, hp)), _const_spec((MLA_KV_LORA, hv)),
                  _const_spec((1, ATT_PAD))],
        out_specs=[tok(hp), tok(hp), tok(hv)],
        out_shape=[jax.ShapeDtypeStruct((t, hp), BF16), jax.ShapeDtypeStruct((t, hp), BF16),
                   jax.ShapeDtypeStruct((t, hv), BF16)],
        compiler_params=_params(1),
    )(x2, pos2, gpre, wlat, qn, kvn, wq, wqr, wk, wv, inv)


def _attn_body(q_ref, k_ref, v_ref, o_ref, s_buf):
    s_len = q_ref.shape[0]
    tq = ATT_Q
    c2 = (MLA_NOPE + MLA_ROPE) ** -0.5 * math.log2(math.e)
    ri = lax.broadcasted_iota(jnp.int32, (tq, tq), 0) // CHUNK
    ci = lax.broadcasted_iota(jnp.int32, (tq, tq), 1) // CHUNK
    diag_mask = ci <= ri
    lane = lax.broadcasted_iota(jnp.int32, (tq, 2 * MLA_V), 1)
    def scores(i, e):
        lo = i * tq
        qrows = slice(lo, lo + tq)
        cols = slice(e * ATT_PAD, (e + 1) * ATT_PAD)
        sb = s_buf.at[2 * (i % 2) + e]
        q = q_ref[qrows, cols]
        sd = jnp.where(diag_mask, _dot_nt(q, k_ref[qrows, cols]), -jnp.inf)
        m = jnp.max(sd, axis=-1, keepdims=True)
        if lo:
            sb[:, 0:lo] = _dot_nt(q, k_ref[0:lo, cols])
            m = jnp.maximum(m, jnp.max(sb[:, 0:lo], axis=-1, keepdims=True))
        return sd, m

    def weighted_values(i, e, sd, m):
        lo = i * tq
        qrows = slice(lo, lo + tq)
        sb = s_buf.at[2 * (i % 2) + e]
        pd = jnp.exp2((sd - m) * c2)
        l = jnp.sum(pd, axis=-1, keepdims=True)
        acc = _dot(pd.astype(BF16), v_ref[qrows, :])
        if lo:
            po = jnp.exp2((sb[:, 0:lo] - m) * c2)
            l = l + jnp.sum(po, axis=-1, keepdims=True)
            acc = acc + _dot(po.astype(BF16), v_ref[0:lo, :])
        return acc / l

    units = [(i, e) for i in range(s_len // tq) for e in range(2)]
    pending = scores(*units[0])
    outs = {}
    for n, (i, e) in enumerate(units):
        sd, m = pending
        if n + 1 < len(units):
            pending = scores(*units[n + 1])
        outs[e] = weighted_values(i, e, sd, m)
        if e == 1:
            qrows = slice(i * tq, (i + 1) * tq)
            o_ref[qrows, :] = jnp.where(lane < MLA_V, outs[0], outs[1]).astype(BF16)


def _attention(q3, k3, v3):
    b, s, _ = q3.shape
    return pl.pallas_call(
        _attn_body,
        name="mla_attn",
        grid=(b, MLA_HEADS // 2),
        in_specs=[pl.BlockSpec((None, s, 2 * ATT_PAD), lambda bi, hp: (bi, 0, hp)),
                  pl.BlockSpec((None, s, 2 * ATT_PAD), lambda bi, hp: (bi, 0, hp)),
                  pl.BlockSpec((None, s, 2 * MLA_V), lambda bi, hp: (bi, 0, hp))],
        out_specs=pl.BlockSpec((None, s, 2 * MLA_V), lambda bi, hp: (bi, 0, hp)),
        out_shape=jax.ShapeDtypeStruct((b, s, MLA_HEADS * MLA_V), BF16),
        scratch_shapes=[pltpu.VMEM((4, ATT_Q, s - ATT_Q), F32)],
        compiler_params=_params(2),
    )(q3, k3, v3)


def _out_proj_body(y_ref, x_ref, w_ref, gpost_ref, o_ref):
    o_ref[...] = x_ref[...] + _rms(_dot(y_ref[...], w_ref[...]), gpost_ref[...])


def _out_proj(y2, x2, w, gpost):
    t, kdim = y2.shape
    tm = OUT_TOKENS
    return pl.pallas_call(
        _out_proj_body,
        name="out_proj",
        grid=(t // tm,),
        in_specs=[pl.BlockSpec((tm, kdim), lambda i: (i, 0)),
                  pl.BlockSpec((tm, D_MODEL), lambda i: (i, 0)),
                  _const_spec((kdim, D_MODEL)), _const_spec((1, D_MODEL))],
        out_specs=pl.BlockSpec((tm, D_MODEL), lambda i: (i, 0)),
        out_shape=jax.ShapeDtypeStruct((t, D_MODEL), F32),
        compiler_params=_params(1),
    )(y2, x2, w, gpost)


def _row(v):
    return v.reshape(1, -1).astype(F32)


def _pad_heads(w, width, offset=0):
    k, hh, d = w.shape
    out = jnp.zeros((k, hh, width), w.dtype)
    out = out.at[:, :, offset:offset + d].set(w)
    return out.reshape(k, hh * width)


def _rot_half_cols(w):
    half = w.shape[-1] // 2
    return jnp.concatenate([-w[..., half:], w[..., :half]], axis=-1)


def _mla_layer(x2, pos2, b, s, gpre, gpost, w_in, q_norm, w_uq, kv_norm, w_ukv, w_o):
    o1 = MLA_Q_LORA
    o2 = o1 + MLA_KV_LORA
    w_pe = w_in[:, o2:]
    slab = jnp.zeros((D_MODEL, ATT_PAD), F32).at[:, MLA_NOPE:MLA_NOPE + MLA_ROPE]
    wlat = jnp.concatenate([w_in[:, :o2], slab.set(w_pe), slab.set(_rot_half_cols(w_pe))],
                           axis=1).astype(BF16)
    wq3 = w_uq.reshape(o1, MLA_HEADS, MLA_NOPE + MLA_ROPE)
    wq = _pad_heads(wq3, ATT_PAD).astype(BF16)
    wqr = _pad_heads(_rot_half_cols(wq3[:, :, MLA_NOPE:]), ATT_PAD, MLA_NOPE).astype(BF16)
    wkv3 = w_ukv.reshape(MLA_KV_LORA, MLA_HEADS, MLA_NOPE + MLA_V)
    wk = _pad_heads(wkv3[:, :, :MLA_NOPE], ATT_PAD).astype(BF16)
    wv = wkv3[:, :, MLA_NOPE:].reshape(MLA_KV_LORA, MLA_HEADS * MLA_V).astype(BF16)
    half = MLA_ROPE // 2
    inv = ROPE_BASE ** (-jnp.arange(half, dtype=F32) / half)
    inv_slab = jnp.zeros((1, ATT_PAD), F32).at[0, MLA_NOPE:MLA_NOPE + MLA_ROPE].set(
        jnp.concatenate([inv, inv]))
    q, k, v = _mla_proj(x2, pos2, _row(gpre), wlat, _row(q_norm), _row(kv_norm), wq, wqr, wk, wv,
                        inv_slab)
    o = _attention(q.reshape(b, s, -1), k.reshape(b, s, -1), v.reshape(b, s, -1))
    return _out_proj(o.reshape(b * s, -1), x2, w_o.astype(BF16), _row(gpost))


def kernel(x, positions, norm_mix_pre, norm_mix_post, norm_ffn_pre, norm_ffn_post, ffn_w_in, ffn_w_out, conv_w_pw1, conv_b_pw1, conv_w_dw, conv_b_dw, conv_ln_g, conv_ln_b, conv_w_pw2, conv_b_pw2, ssm_w_in, ssm_conv_w, ssm_conv_b, ssm_dt_bias, ssm_a_log, ssm_d, ssm_norm_w, ssm_w_out, mla_w_in, mla_q_norm, mla_w_uq, mla_kv_norm, mla_w_ukv, mla_w_o):
    b, s, d = x.shape
    t = b * s
    pos2 = positions.reshape(t, 1)
    ffn_w1 = ffn_w_in.astype(BF16)
    ffn_w2 = ffn_w_out.astype(BF16)
    conv_w1 = conv_w_pw1.astype(BF16)
    conv_w2 = conv_w_pw2.astype(BF16)
    i_conv = i_ssm = i_mla = 0
    for i in range(DEPTH):
        kind = i % N_MIXERS
        gpre, gpost = _row(norm_mix_pre[i]), _row(norm_mix_post[i])
        if kind == 0:
            jx = i_conv
            x = _conv_layer(
                x.reshape(b, s, d), gpre, gpost, conv_w1, _row(conv_b_pw1[jx]),
                conv_w_dw[jx], _row(conv_b_dw[jx]), _row(conv_ln_g[jx]), _row(conv_ln_b[jx]),
                conv_w2, _row(conv_b_pw2[jx]), jx)
            i_conv += 1
        elif kind == 1:
            jx = i_ssm
            w_in = ssm_w_in[jx]
            o1 = SSM_D_INNER
            o2 = o1 + SSM_CONV_DIM
            rep = SSM_HEAD_DIM
            pad = LANES - SSM_HEADS
            expand = jnp.repeat(jnp.eye(LANES, SSM_HEADS, dtype=BF16), rep, axis=1)
            x = _ssd_layer(
                x.reshape(b, s, d), gpre, gpost, w_in[:, :o1].astype(BF16),
                w_in[:, o1:o2].astype(BF16), jnp.pad(w_in[:, o2:], ((0, 0), (0, pad))).astype(BF16),
                ssm_conv_w[jx], _row(ssm_conv_b[jx]), _row(jnp.pad(ssm_dt_bias[jx], (0, pad))),
                expand, _row(jnp.pad(ssm_a_log[jx], (0, pad))),
                _row(jnp.repeat(ssm_a_log[jx], rep)), _row(jnp.repeat(ssm_d[jx], rep)),
                _row(ssm_norm_w[jx]), ssm_w_out[jx].astype(BF16))
            i_ssm += 1
        else:
            jx = i_mla
            x = _mla_layer(x.reshape(t, d), pos2, b, s, norm_mix_pre[i], norm_mix_post[i],
                           mla_w_in[jx], mla_q_norm[jx], mla_w_uq[jx], mla_kv_norm[jx],
                           mla_w_ukv[jx], mla_w_o[jx])
            i_mla += 1
        x = _ffn(x.reshape(t, d), _row(norm_ffn_pre[i]), _row(norm_ffn_post[i]), ffn_w1, ffn_w2, i)
    return x.reshape(b, s, d)
```

```python
import math

import jax
import jax.numpy as jnp
from jax import lax
from jax.experimental import pallas as pl
from jax.experimental.pallas import tpu as pltpu

F32 = jnp.float32
BF16 = jnp.bfloat16

D_MODEL = 1024
DEPTH = 4
CHUNK = 64
N_MIXERS = 3
EPS = 1e-6
FFN_DIM = 4 * D_MODEL
CONV_KERNEL = 31
SSM_D_INNER = 2 * D_MODEL
SSM_HEAD_DIM = 64
SSM_HEADS = SSM_D_INNER // SSM_HEAD_DIM
SSM_GROUPS = 8
SSM_STATE = 128
SSM_CONV = 4
SSM_CONV_DIM = SSM_D_INNER + 2 * SSM_GROUPS * SSM_STATE
MLA_HEADS = D_MODEL // 64
MLA_NOPE = 64
MLA_ROPE = 32
MLA_V = 64
MLA_Q_LORA = 3 * D_MODEL // 8
MLA_KV_LORA = D_MODEL // 4
ROPE_BASE = 10000.0

LANES = 128
SUBLANES = 8
VMEM_LIMIT_BYTES = 56 * 1024 * 1024

FFN_TOKENS = 1024
FFN_SUB = 512
FFN_CHUNK = 512
CONV_TOKENS = 512
CONV_HALO = 32
CONV_ROWS = 32
CONV_LN_ROWS = 128
CONV_LANES = 256
SSD_TOKENS = 256
SSD_CHUNK = 128
SSD_HALO = SUBLANES
SSD_CONV_LANES = 512
MLA_TOKENS = 512
ATT_Q = 256
ATT_PAD = 128
ATT_AHEAD = 2
OUT_TOKENS = 512


def _const_spec(shape):
    nd = len(shape)
    return pl.BlockSpec(shape, lambda *_: (0,) * nd, pipeline_mode=pl.Buffered(1))


def _layer_spec(shape, layer):
    nd = len(shape)
    return pl.BlockSpec((None,) + tuple(shape), lambda *_: (layer,) + (0,) * nd,
                        pipeline_mode=pl.Buffered(1))


def _params(n_axes, parallel=True):
    sem = ("parallel" if parallel else "arbitrary",) * n_axes
    return pltpu.CompilerParams(dimension_semantics=sem, vmem_limit_bytes=VMEM_LIMIT_BYTES)


def _rms(x, g):
    return x * lax.rsqrt(jnp.mean(x * x, axis=-1, keepdims=True) + EPS) * g


def _dot(a, b):
    return jnp.dot(a, b, preferred_element_type=F32)


def _dot_nt(a, b):
    return lax.dot_general(a, b, (((1,), (1,)), ((), ())), preferred_element_type=F32)


def _dot_tn(a, b):
    return lax.dot_general(a, b, (((0,), (0,)), ((), ())), preferred_element_type=F32)


def _split3(x):
    hi = x.astype(BF16)
    r1 = x - hi.astype(F32)
    mid = r1.astype(BF16)
    lo = (r1 - mid.astype(F32)).astype(BF16)
    return hi, mid, lo


def _dot_exact01(m01, x):
    hi, mid, lo = _split3(x)
    return (_dot(jnp.concatenate([m01, m01], axis=1), jnp.concatenate([hi, mid], axis=0))
            + _dot(m01, lo))


def _dot_exact01_rhs(x, m01):
    hi, mid, lo = _split3(x)
    return (_dot(jnp.concatenate([hi, mid], axis=1), jnp.concatenate([m01, m01], axis=0))
            + _dot(lo, m01))


def _ffn_body(x_ref, gpre_ref, gpost_ref, w1_ref, w2_ref, o_ref):
    for sub in range(FFN_TOKENS // FFN_SUB):
        rows = slice(sub * FFN_SUB, (sub + 1) * FFN_SUB)
        x = x_ref[rows, :]
        h = _rms(x, gpre_ref[...]).astype(BF16)
        acc = None
        for c in range(FFN_DIM // FFN_CHUNK):
            cols = slice(c * FFN_CHUNK, (c + 1) * FFN_CHUNK)
            a = _dot(h, w1_ref[:, cols])
            a = jnp.square(jnp.maximum(a, 0.0)).astype(BF16)
            p = _dot(a, w2_ref[cols, :])
            acc = p if acc is None else acc + p
        o_ref[rows, :] = x + _rms(acc, gpost_ref[...])


def _ffn(x2, gpre, gpost, w1_all, w2_all, layer):
    t = x2.shape[0]
    tok = pl.BlockSpec((FFN_TOKENS, D_MODEL), lambda i: (i, 0))
    return pl.pallas_call(
        _ffn_body,
        name="ffn",
        grid=(t // FFN_TOKENS,),
        in_specs=[tok, _const_spec((1, D_MODEL)), _const_spec((1, D_MODEL)),
                  _layer_spec((D_MODEL, FFN_DIM), layer), _layer_spec((FFN_DIM, D_MODEL), layer)],
        out_specs=tok,
        out_shape=jax.ShapeDtypeStruct((t, D_MODEL), F32),
        compiler_params=_params(1),
    )(x2, gpre, gpost, w1_all, w2_all)


def _conv_body(x_ref, gpre_ref, gpost_ref, w1_ref, b1_ref, wdw_ref, bdw_ref,
               lng_ref, lnb_ref, w2_ref, b2_ref, o_ref, ubuf, shifted, dbuf):
    ts = x_ref.shape[0]
    j = pl.program_id(1)

    @pl.when(j == 0)
    def _():
        ubuf[0:CONV_HALO, :] = jnp.zeros((CONV_HALO, D_MODEL), F32)

    @pl.when(j > 0)
    def _():
        ubuf[0:CONV_HALO, :] = ubuf[ts:ts + CONV_HALO, :]

    x = x_ref[...]
    h = _rms(x, gpre_ref[...]).astype(BF16)
    first = CONV_HALO - (CONV_KERNEL - 1)
    n_sh = CONV_HALO + ts - SUBLANES

    for cblk in range(D_MODEL // CONV_LANES):
        cs = slice(cblk * CONV_LANES, (cblk + 1) * CONV_LANES)
        gs = slice(D_MODEL + cblk * CONV_LANES, D_MODEL + (cblk + 1) * CONV_LANES)
        ua = _dot(h, w1_ref[:, cs]) + b1_ref[:, cs]
        ub = _dot(h, w1_ref[:, gs]) + b1_ref[:, gs]
        ubuf[CONV_HALO:CONV_HALO + ts, cs] = ua * jax.nn.sigmoid(ub)
        window = ubuf[:, cs]
        for sft in range(1, SUBLANES):
            rolled = pltpu.roll(window, CONV_HALO + ts - sft, axis=0)
            shifted[sft - 1, 0:n_sh, cs] = rolled[0:n_sh, :]
        for sub in range(CONV_LANES // LANES):
            ls = slice(cblk * CONV_LANES + sub * LANES, cblk * CONV_LANES + (sub + 1) * LANES)
            for rb in range(ts // CONV_ROWS):
                acc = jnp.broadcast_to(bdw_ref[:, ls], (CONV_ROWS, LANES))
                for k in range(CONV_KERNEL):
                    lo = rb * CONV_ROWS + first + k
                    sft = lo % SUBLANES
                    if sft:
                        tap = shifted[sft - 1, lo - sft:lo - sft + CONV_ROWS, ls]
                    else:
                        tap = ubuf[lo:lo + CONV_ROWS, ls]
                    acc = acc + wdw_ref[k:k + 1, ls] * tap
                dbuf[rb * CONV_ROWS:(rb + 1) * CONV_ROWS, ls] = acc

    for rt in range(ts // CONV_LN_ROWS):
        rows = slice(rt * CONV_LN_ROWS, (rt + 1) * CONV_LN_ROWS)
        acc = dbuf[rows, :]
        mu = jnp.mean(acc, axis=-1, keepdims=True)
        cen = acc - mu
        var = jnp.mean(cen * cen, axis=-1, keepdims=True)
        y = cen * lax.rsqrt(var + EPS) * lng_ref[...] + lnb_ref[...]
        v = (y * jax.nn.sigmoid(y)).astype(BF16)
        out = _dot(v, w2_ref[...]) + b2_ref[...]
        o_ref[rows, :] = x[rows, :] + _rms(out, gpost_ref[...])


def _conv_layer(x3, gpre, gpost, w1_all, b1, wdw, bdw, lng, lnb, w2_all, b2, layer):
    b, s, _ = x3.shape
    ts = CONV_TOKENS
    tok = pl.BlockSpec((None, ts, D_MODEL), lambda bi, j: (bi, j, 0))
    vec = _const_spec((1, D_MODEL))
    return pl.pallas_call(
        _conv_body,
        name="conv_mixer",
        grid=(b, s // ts),
        in_specs=[tok, vec, vec, _layer_spec((D_MODEL, 2 * D_MODEL), layer),
                  _const_spec((1, 2 * D_MODEL)), _const_spec((CONV_KERNEL, D_MODEL)), vec,
                  vec, vec, _layer_spec((D_MODEL, D_MODEL), layer), vec],
        out_specs=tok,
        out_shape=jax.ShapeDtypeStruct(x3.shape, F32),
        scratch_shapes=[pltpu.VMEM((CONV_HALO + ts, D_MODEL), F32),
                        pltpu.VMEM((SUBLANES - 1, CONV_HALO + ts, D_MODEL), F32),
                        pltpu.VMEM((ts, D_MODEL), F32)],
        compiler_params=_params(2, parallel=False),
    )(x3, gpre, gpost, w1_all, b1, wdw, bdw, lng, lnb, w2_all, b2)


def _ssd_body(x_ref, gpre_ref, gpost_ref, win_ref, wdt_ref, cw_ref, cb_ref, dtb_ref,
              expand_ref, alogh_ref, alog_ref, dskip_ref, nw_ref, wout_ref, o_ref, cbuf, xbuf,
              dtbuf, zbuf, ybuf, state):
    ts = x_ref.shape[0]
    lc = SSD_CHUNK
    gw = SSM_D_INNER // SSM_GROUPS
    j = pl.program_id(1)

    @pl.when(j == 0)
    def _():
        cbuf[0:SSD_HALO, :] = jnp.zeros((SSD_HALO, SSM_CONV_DIM), F32)
        state[...] = jnp.zeros(state.shape, F32)

    @pl.when(j > 0)
    def _():
        cbuf[0:SSD_HALO, :] = cbuf[ts:ts + SSD_HALO, :]

    x = x_ref[...]
    h = _rms(x, gpre_ref[...]).astype(BF16)
    dt_raw = _dot(h, wdt_ref[...]) + dtb_ref[...]
    dt_heads = jnp.maximum(dt_raw, 0.0) + jnp.log1p(jnp.exp(-jnp.abs(dt_raw)))
    dtbuf[...] = _dot_exact01_rhs(dt_heads, expand_ref[...])
    zbuf[...] = _dot(h, win_ref[:, 0:SSM_D_INNER])

    for cblk in range(SSM_CONV_DIM // SSD_CONV_LANES):
        cs = slice(cblk * SSD_CONV_LANES, (cblk + 1) * SSD_CONV_LANES)
        ws = slice(SSM_D_INNER + cblk * SSD_CONV_LANES, SSM_D_INNER + (cblk + 1) * SSD_CONV_LANES)
        cbuf[SSD_HALO:SSD_HALO + ts, cs] = _dot(h, win_ref[:, ws])
        window = cbuf[0:SSD_HALO + ts, cs]
        conv = jnp.broadcast_to(cb_ref[:, cs], (ts, SSD_CONV_LANES))
        for k in range(SSM_CONV):
            back = SSM_CONV - 1 - k
            tap = pltpu.roll(window, back, axis=0) if back else window
            conv = conv + cw_ref[k:k + 1, cs] * tap[SSD_HALO:SSD_HALO + ts, :]
        xbuf[:, cs] = conv * jax.nn.sigmoid(conv)

    a_row = -jnp.exp(alog_ref[...])
    a_heads = -jnp.exp(alogh_ref[...])
    ri = lax.broadcasted_iota(jnp.int32, (lc, lc), 0)
    ci = lax.broadcasted_iota(jnp.int32, (lc, lc), 1)
    causal = ci <= ri
    tril01 = causal.astype(BF16)
    off_b = SSM_D_INNER
    off_c = SSM_D_INNER + SSM_GROUPS * SSM_STATE
    lane = lax.broadcasted_iota(jnp.int32, (1, gw), 1)
    head01 = [((lane >= r * SSM_HEAD_DIM) & (lane < (r + 1) * SSM_HEAD_DIM)).astype(BF16)
              for r in range(gw // SSM_HEAD_DIM)]

    for c in range(ts // lc):
        rows = slice(c * lc, (c + 1) * lc)
        dt = dtbuf[rows, :]
        a_cs = _dot_exact01(tril01, dt * a_row)
        a_last = a_cs[lc - 1:lc, :]
        exp_a = jnp.exp(a_cs)
        to_end = jnp.exp(a_last - a_cs)
        decay_chunk = jnp.exp(a_last)
        a_csh = _dot_exact01(tril01, dt_heads[rows, :] * a_heads)
        a_csh_t = a_csh.T
        for g in range(SSM_GROUPS):
            gl = slice(g * gw, (g + 1) * gw)
            xs = xbuf[rows, gl]
            xdt = xs * dt[:, gl]
            xdb = xdt.astype(BF16)
            bm = xbuf[rows, off_b + g * SSM_STATE:off_b + (g + 1) * SSM_STATE].astype(BF16)
            cm = xbuf[rows, off_c + g * SSM_STATE:off_c + (g + 1) * SSM_STATE].astype(BF16)
            cb = _dot_nt(cm, bm)
            ms, xds = [], []
            for r in range(gw // SSM_HEAD_DIM):
                hd = g * (gw // SSM_HEAD_DIM) + r
                col = a_csh[:, hd:hd + 1]
                row = a_csh_t[hd:hd + 1, :]
                decay = jnp.where(causal, jnp.exp(col - row), 0.0)
                ms.append((cb * decay).astype(BF16))
                xds.append(xdb * head01[r])
            y = (_dot(jnp.concatenate(ms[0:2], axis=1), jnp.concatenate(xds[0:2], axis=0))
                 + _dot(jnp.concatenate(ms[2:4], axis=1), jnp.concatenate(xds[2:4], axis=0)))
            st = state[g]
            y = y + _dot(cm, st.astype(BF16)) * exp_a[:, gl]
            xw = (xdt * to_end[:, gl]).astype(BF16)
            state[g] = st * decay_chunk[:, gl] + _dot_tn(bm, xw)
            y = y + dskip_ref[:, gl] * xs
            zg = zbuf[rows, gl]
            y = y * (zg * jax.nn.sigmoid(zg))
            y = y * lax.rsqrt(jnp.mean(y * y, axis=-1, keepdims=True) + EPS) * nw_ref[:, gl]
            ybuf[rows, gl] = y.astype(BF16)
        out = _dot(ybuf[rows, :], wout_ref[...])
        o_ref[rows, :] = x[rows, :] + _rms(out, gpost_ref[...])


def _ssd_layer(x3, gpre, gpost, win, wdt, cw, cb, dtb, expand, alogh, alog, dskip, nw, wout):
    b, s, _ = x3.shape
    ts = SSD_TOKENS
    tok = pl.BlockSpec((None, ts, D_MODEL), lambda bi, j: (bi, j, 0))
    vec = _const_spec((1, D_MODEL))
    inner = _const_spec((1, SSM_D_INNER))
    return pl.pallas_call(
        _ssd_body,
        name="ssd_mixer",
        grid=(b, s // ts),
        in_specs=[tok, vec, vec, _const_spec(win.shape), _const_spec((D_MODEL, LANES)),
                  _const_spec((SSM_CONV, SSM_CONV_DIM)), _const_spec((1, SSM_CONV_DIM)),
                  _const_spec((1, LANES)), _const_spec((LANES, SSM_D_INNER)), _const_spec((1, LANES)),
                  inner, inner, inner, _const_spec((SSM_D_INNER, D_MODEL))],
        out_specs=tok,
        out_shape=jax.ShapeDtypeStruct(x3.shape, F32),
        scratch_shapes=[pltpu.VMEM((SSD_HALO + ts, SSM_CONV_DIM), F32),
                        pltpu.VMEM((ts, SSM_CONV_DIM), F32),
                        pltpu.VMEM((ts, SSM_D_INNER), F32),
                        pltpu.VMEM((ts, SSM_D_INNER), F32),
                        pltpu.VMEM((ts, SSM_D_INNER), BF16),
                        pltpu.VMEM((SSM_GROUPS, SSM_STATE, SSM_D_INNER // SSM_GROUPS), F32)],
        compiler_params=_params(2, parallel=False),
    )(x3, gpre, gpost, win, wdt, cw, cb, dtb, expand, alogh, alog, dskip, nw, wout)


def _mla_proj_body(x_ref, pos_ref, gpre_ref, wlat_ref, qn_ref, kvn_ref, wq_ref, wqr_ref, wk_ref,
                   wv_ref, inv_ref, q_ref, k_ref, v_ref):
    x = x_ref[...]
    h = _rms(x, gpre_ref[...]).astype(BF16)
    lat = _dot(h, wlat_ref[...])
    o1 = MLA_Q_LORA
    o2 = o1 + MLA_KV_LORA
    cq = _rms(lat[:, :o1], qn_ref[...]).astype(BF16)
    ckv = _rms(lat[:, o1:o2], kvn_ref[...]).astype(BF16)
    pe = lat[:, o2:o2 + ATT_PAD]
    pe_rot = lat[:, o2 + ATT_PAD:o2 + 2 * ATT_PAD]
    ang = pos_ref[...].astype(F32) * inv_ref[...]
    cos = jnp.cos(ang)
    sin = jnp.sin(ang)
    reps = (1, MLA_HEADS)
    k_pe = pe * cos + pe_rot * sin
    k_ref[...] = (_dot(ckv, wk_ref[...]) + jnp.tile(k_pe, reps)).astype(BF16)
    q = _dot(cq, wq_ref[...]) * jnp.tile(cos, reps) + _dot(cq, wqr_ref[...]) * jnp.tile(sin, reps)
    q_ref[...] = q.astype(BF16)
    v_ref[...] = _dot(ckv, wv_ref[...]).astype(BF16)


def _mla_proj(x2, pos2, gpre, wlat, qn, kvn, wq, wqr, wk, wv, inv):
    t = x2.shape[0]
    tm = MLA_TOKENS
    hp = MLA_HEADS * ATT_PAD
    hv = MLA_HEADS * MLA_V

    def tok(w):
        return pl.BlockSpec((tm, w), lambda i: (i, 0))

    return pl.pallas_call(
        _mla_proj_body,
        name="mla_proj",
        grid=(t // tm,),
        in_specs=[tok(D_MODEL), tok(1), _const_spec((1, D_MODEL)), _const_spec(wlat.shape),
                  _const_spec((1, MLA_Q_LORA)), _const_spec((1, MLA_KV_LORA)),
                  _const_spec((MLA_Q_LORA, hp)), _const_spec((MLA_Q_LORA, hp)),
                  _const_spec((MLA_KV_LORA, hp)), _const_spec((MLA_KV_LORA, hv)),
                  _const_spec((1, ATT_PAD))],
        out_specs=[tok(hp), tok(hp), tok(hv)],
        out_shape=[jax.ShapeDtypeStruct((t, hp), BF16), jax.ShapeDtypeStruct((t, hp), BF16),
                   jax.ShapeDtypeStruct((t, hv), BF16)],
        compiler_params=_params(1),
    )(x2, pos2, gpre, wlat, qn, kvn, wq, wqr, wk, wv, inv)


def _attn_body(q_ref, k_ref, v_ref, o_ref, s_buf):
    s_len = q_ref.shape[0]
    tq = ATT_Q
    c2 = (MLA_NOPE + MLA_ROPE) ** -0.5 * math.log2(math.e)
    ri = lax.broadcasted_iota(jnp.int32, (tq, tq), 0) // CHUNK
    ci = lax.broadcasted_iota(jnp.int32, (tq, tq), 1) // CHUNK
    diag_mask = ci <= ri
    lane = lax.broadcasted_iota(jnp.int32, (tq, 2 * MLA_V), 1)

    def scores(i, e):
        lo = i * tq
        qrows = slice(lo, lo + tq)
        cols = slice(e * ATT_PAD, (e + 1) * ATT_PAD)
        sb = s_buf.at[2 * (i % 2) + e]
        q = q_ref[qrows, cols]
        sd = jnp.where(diag_mask, _dot_nt(q, k_ref[qrows, cols]), -jnp.inf)
        m = jnp.max(sd, axis=-1, keepdims=True)
        if lo:
            sb[:, 0:lo] = _dot_nt(q, k_ref[0:lo, cols])
            m = jnp.maximum(m, jnp.max(sb[:, 0:lo], axis=-1, keepdims=True))
        return sd, m

    def weighted_values(i, e, sd, m):
        lo = i * tq
        qrows = slice(lo, lo + tq)
        sb = s_buf.at[2 * (i % 2) + e]
        pd = jnp.exp2((sd - m) * c2)
        l = jnp.sum(pd, axis=-1, keepdims=True)
        acc = _dot(pd.astype(BF16), v_ref[qrows, :])
        if lo:
            po = jnp.exp2((sb[:, 0:lo] - m) * c2)
            l = l + jnp.sum(po, axis=-1, keepdims=True)
            acc = acc + _dot(po.astype(BF16), v_ref[0:lo, :])
        return acc / l

    units = [(i, e) for i in range(s_len // tq) for e in range(2)]
    pending = [scores(*u) for u in units[:ATT_AHEAD]]
    outs = {}
    for n, (i, e) in enumerate(units):
        sd, m = pending.pop(0)
        if n + ATT_AHEAD < len(units):
            pending.append(scores(*units[n + ATT_AHEAD]))
        outs[e] = weighted_values(i, e, sd, m)
        if e == 1:
            qrows = slice(i * tq, (i + 1) * tq)
            o_ref[qrows, :] = jnp.where(lane < MLA_V, outs[0], outs[1]).astype(BF16)


def _attention(q3, k3, v3):
    b, s, _ = q3.shape
    return pl.pallas_call(
        _attn_body,
        name="mla_attn",
        grid=(b, MLA_HEADS // 2),
        in_specs=[pl.BlockSpec((None, s, 2 * ATT_PAD), lambda bi, hp: (bi, 0, hp)),
                  pl.BlockSpec((None, s, 2 * ATT_PAD), lambda bi, hp: (bi, 0, hp)),
                  pl.BlockSpec((None, s, 2 * MLA_V), lambda bi, hp: (bi, 0, hp))],
        out_specs=pl.BlockSpec((None, s, 2 * MLA_V), lambda bi, hp: (bi, 0, hp)),
        out_shape=jax.ShapeDtypeStruct((b, s, MLA_HEADS * MLA_V), BF16),
        scratch_shapes=[pltpu.VMEM((4, ATT_Q, s - ATT_Q), F32)],
        compiler_params=_params(2),
    )(q3, k3, v3)


def _out_proj_body(y_ref, x_ref, w_ref, gpost_ref, o_ref):
    o_ref[...] = x_ref[...] + _rms(_dot(y_ref[...], w_ref[...]), gpost_ref[...])


def _out_proj(y2, x2, w, gpost):
    t, kdim = y2.shape
    tm = OUT_TOKENS
    return pl.pallas_call(
        _out_proj_body,
        name="out_proj",
        grid=(t // tm,),
        in_specs=[pl.BlockSpec((tm, kdim), lambda i: (i, 0)),
                  pl.BlockSpec((tm, D_MODEL), lambda i: (i, 0)),
                  _const_spec((kdim, D_MODEL)), _const_spec((1, D_MODEL))],
        out_specs=pl.BlockSpec((tm, D_MODEL), lambda i: (i, 0)),
        out_shape=jax.ShapeDtypeStruct((t, D_MODEL), F32),
        compiler_params=_params(1),
    )(y2, x2, w, gpost)


def _row(v):
    return v.reshape(1, -1).astype(F32)


def _pad_heads(w, width, offset=0):
    k, hh, d = w.shape
    out = jnp.zeros((k, hh, width), w.dtype)
    out = out.at[:, :, offset:offset + d].set(w)
    return out.reshape(k, hh * width)


def _rot_half_cols(w):
    half = w.shape[-1] // 2
    return jnp.concatenate([-w[..., half:], w[..., :half]], axis=-1)


def _mla_layer(x2, pos2, b, s, gpre, gpost, w_in, q_norm, w_uq, kv_norm, w_ukv, w_o):
    o1 = MLA_Q_LORA
    o2 = o1 + MLA_KV_LORA
    w_pe = w_in[:, o2:]
    slab = jnp.zeros((D_MODEL, ATT_PAD), F32).at[:, MLA_NOPE:MLA_NOPE + MLA_ROPE]
    wlat = jnp.concatenate([w_in[:, :o2], slab.set(w_pe), slab.set(_rot_half_cols(w_pe))],
                           axis=1).astype(BF16)
    wq3 = w_uq.reshape(o1, MLA_HEADS, MLA_NOPE + MLA_ROPE)
    wq = _pad_heads(wq3, ATT_PAD).astype(BF16)
    wqr = _pad_heads(_rot_half_cols(wq3[:, :, MLA_NOPE:]), ATT_PAD, MLA_NOPE).astype(BF16)
    wkv3 = w_ukv.reshape(MLA_KV_LORA, MLA_HEADS, MLA_NOPE + MLA_V)
    wk = _pad_heads(wkv3[:, :, :MLA_NOPE], ATT_PAD).astype(BF16)
    wv = wkv3[:, :, MLA_NOPE:].reshape(MLA_KV_LORA, MLA_HEADS * MLA_V).astype(BF16)
    half = MLA_ROPE // 2
    inv = ROPE_BASE ** (-jnp.arange(half, dtype=F32) / half)
    inv_slab = jnp.zeros((1, ATT_PAD), F32).at[0, MLA_NOPE:MLA_NOPE + MLA_ROPE].set(
        jnp.concatenate([inv, inv]))
    q, k, v = _mla_proj(x2, pos2, _row(gpre), wlat, _row(q_norm), _row(kv_norm), wq, wqr, wk, wv,
                        inv_slab)
    o = _attention(q.reshape(b, s, -1), k.reshape(b, s, -1), v.reshape(b, s, -1))
    return _out_proj(o.reshape(b * s, -1), x2, w_o.astype(BF16), _row(gpost))


def kernel(x, positions, norm_mix_pre, norm_mix_post, norm_ffn_pre, norm_ffn_post, ffn_w_in, ffn_w_out, conv_w_pw1, conv_b_pw1, conv_w_dw, conv_b_dw, conv_ln_g, conv_ln_b, conv_w_pw2, conv_b_pw2, ssm_w_in, ssm_conv_w, ssm_conv_b, ssm_dt_bias, ssm_a_log, ssm_d, ssm_norm_w, ssm_w_out, mla_w_in, mla_q_norm, mla_w_uq, mla_kv_norm, mla_w_ukv, mla_w_o):
    b, s, d = x.shape
    t = b * s
    pos2 = positions.reshape(t, 1)
    ffn_w1 = ffn_w_in.astype(BF16)
    ffn_w2 = ffn_w_out.astype(BF16)
    conv_w1 = conv_w_pw1.astype(BF16)
    conv_w2 = conv_w_pw2.astype(BF16)
    i_conv = i_ssm = i_mla = 0
    for i in range(DEPTH):
        kind = i % N_MIXERS
        gpre, gpost = _row(norm_mix_pre[i]), _row(norm_mix_post[i])
        if kind == 0:
            jx = i_conv
            x = _conv_layer(
                x.reshape(b, s, d), gpre, gpost, conv_w1, _row(conv_b_pw1[jx]),
                conv_w_dw[jx], _row(conv_b_dw[jx]), _row(conv_ln_g[jx]), _row(conv_ln_b[jx]),
                conv_w2, _row(conv_b_pw2[jx]), jx)
            i_conv += 1
        elif kind == 1:
            jx = i_ssm
            w_in = ssm_w_in[jx].astype(BF16)
            o2 = SSM_D_INNER + SSM_CONV_DIM
            rep = SSM_HEAD_DIM
            pad = LANES - SSM_HEADS
            expand = jnp.repeat(jnp.eye(LANES, SSM_HEADS, dtype=BF16), rep, axis=1)
            x = _ssd_layer(
                x.reshape(b, s, d), gpre, gpost, w_in, jnp.pad(w_in[:, o2:], ((0, 0), (0, pad))),
                ssm_conv_w[jx], _row(ssm_conv_b[jx]), _row(jnp.pad(ssm_dt_bias[jx], (0, pad))),
                expand, _row(jnp.pad(ssm_a_log[jx], (0, pad))),
                _row(jnp.repeat(ssm_a_log[jx], rep)), _row(jnp.repeat(ssm_d[jx], rep)),
                _row(ssm_norm_w[jx]), ssm_w_out[jx].astype(BF16))
            i_ssm += 1
        else:
            jx = i_mla
            x = _mla_layer(x.reshape(t, d), pos2, b, s, norm_mix_pre[i], norm_mix_post[i],
                           mla_w_in[jx], mla_q_norm[jx], mla_w_uq[jx], mla_kv_norm[jx],
                           mla_w_ukv[jx], mla_w_o[jx])
            i_mla += 1
        x = _ffn(x.reshape(t, d), _row(norm_ffn_pre[i]), _row(norm_ffn_post[i]), ffn_w1, ffn_w2, i)
    return x.reshape(b, s, d)
```

```python
import math

import jax
import jax.numpy as jnp
from jax import lax
from jax.experimental import pallas as pl
from jax.experimental.pallas import tpu as pltpu

F32 = jnp.float32
BF16 = jnp.bfloat16

D_MODEL = 1024
DEPTH = 4
CHUNK = 64
N_MIXERS = 3
EPS = 1e-6
FFN_DIM = 4 * D_MODEL
CONV_KERNEL = 31
SSM_D_INNER = 2 * D_MODEL
SSM_HEAD_DIM = 64
SSM_HEADS = SSM_D_INNER // SSM_HEAD_DIM
SSM_GROUPS = 8
SSM_STATE = 128
SSM_CONV = 4
SSM_CONV_DIM = SSM_D_INNER + 2 * SSM_GROUPS * SSM_STATE
MLA_HEADS = D_MODEL // 64
MLA_NOPE = 64
MLA_ROPE = 32
MLA_V = 64
MLA_Q_LORA = 3 * D_MODEL // 8
MLA_KV_LORA = D_MODEL // 4
ROPE_BASE = 10000.0
LOG2E = math.log2(math.e)

LANES = 128
SUBLANES = 8
VMEM_LIMIT_BYTES = 56 * 1024 * 1024

FFN_TOKENS = 1024
FFN_SUB = 512
FFN_CHUNK = 512
CONV_TOKENS = 512
CONV_HALO = 32
CONV_ROWS = 32
CONV_LN_ROWS = 128
CONV_LANES = 256
SSD_TOKENS = 256
SSD_CHUNK = 128
SSD_HALO = SUBLANES
SSD_CONV_LANES = 512
SSD_AHEAD = 1
MLA_TOKENS = 512
MLA_SUB = 256
ATT_Q = 256
ATT_PAD = 128
ATT_AHEAD = 2
OUT_TOKENS = 512


def _const_spec(shape):
    nd = len(shape)
    return pl.BlockSpec(shape, lambda *_: (0,) * nd, pipeline_mode=pl.Buffered(1))


def _layer_spec(shape, layer):
    nd = len(shape)
    return pl.BlockSpec((None,) + tuple(shape), lambda *_: (layer,) + (0,) * nd,
                        pipeline_mode=pl.Buffered(1))


def _params(n_axes, parallel=True):
    sem = ("parallel" if parallel else "arbitrary",) * n_axes
    return pltpu.CompilerParams(dimension_semantics=sem, vmem_limit_bytes=VMEM_LIMIT_BYTES)


def _rms(x, g):
    return x * lax.rsqrt(jnp.mean(x * x, axis=-1, keepdims=True) + EPS) * g


def _silu(x):
    half = 0.5 * x
    return half + half * jnp.tanh(half)


def _dot(a, b):
    return jnp.dot(a, b, preferred_element_type=F32)


def _dot_nt(a, b):
    return lax.dot_general(a, b, (((1,), (1,)), ((), ())), preferred_element_type=F32)


def _dot_tn(a, b):
    return lax.dot_general(a, b, (((0,), (0,)), ((), ())), preferred_element_type=F32)


def _split3(x):
    hi = x.astype(BF16)
    r1 = x - hi.astype(F32)
    mid = r1.astype(BF16)
    lo = (r1 - mid.astype(F32)).astype(BF16)
    return hi, mid, lo


def _dot_exact01(m01, x):
    hi, mid, lo = _split3(x)
    return (_dot(jnp.concatenate([m01, m01], axis=1), jnp.concatenate([hi, mid], axis=0))
            + _dot(m01, lo))


def _dot_exact01_rhs(x, m01):
    hi, mid, lo = _split3(x)
    return (_dot(jnp.concatenate([hi, mid], axis=1), jnp.concatenate([m01, m01], axis=0))
            + _dot(lo, m01))


def _ffn_body(x_ref, gpre_ref, gpost_ref, w1_ref, w2_ref, o_ref):
    for sub in range(FFN_TOKENS // FFN_SUB):
        rows = slice(sub * FFN_SUB, (sub + 1) * FFN_SUB)
        x = x_ref[rows, :]
        h = _rms(x, gpre_ref[...]).astype(BF16)
        acc = None
        for c in range(FFN_DIM // FFN_CHUNK):
            cols = slice(c * FFN_CHUNK, (c + 1) * FFN_CHUNK)
            a = _dot(h, w1_ref[:, cols])
            a = jnp.square(jnp.maximum(a, 0.0)).astype(BF16)
            p = _dot(a, w2_ref[cols, :])
            acc = p if acc is None else acc + p
        o_ref[rows, :] = x + _rms(acc, gpost_ref[...])


def _ffn(x2, gpre, gpost, w1_all, w2_all, layer):
    t = x2.shape[0]
    tok = pl.BlockSpec((FFN_TOKENS, D_MODEL), lambda i: (i, 0))
    return pl.pallas_call(
        _ffn_body,
        name="ffn",
        grid=(t // FFN_TOKENS,),
        in_specs=[tok, _const_spec((1, D_MODEL)), _const_spec((1, D_MODEL)),
                  _layer_spec((D_MODEL, FFN_DIM), layer), _layer_spec((FFN_DIM, D_MODEL), layer)],
        out_specs=tok,
        out_shape=jax.ShapeDtypeStruct((t, D_MODEL), F32),
        compiler_params=_params(1),
    )(x2, gpre, gpost, w1_all, w2_all)


def _conv_body(x_ref, gpre_ref, gpost_ref, w1_ref, b1_ref, wdw_ref, bdw_ref,
               lng_ref, lnb_ref, w2_ref, b2_ref, o_ref, ubuf, shifted, dbuf):
    ts = x_ref.shape[0]
    j = pl.program_id(1)

    @pl.when(j == 0)
    def _():
        ubuf[0:CONV_HALO, :] = jnp.zeros((CONV_HALO, D_MODEL), F32)

    @pl.when(j > 0)
    def _():
        ubuf[0:CONV_HALO, :] = ubuf[ts:ts + CONV_HALO, :]

    x = x_ref[...]
    h = _rms(x, gpre_ref[...]).astype(BF16)
    first = CONV_HALO - (CONV_KERNEL - 1)
    n_sh = CONV_HALO + ts - SUBLANES

    for cblk in range(D_MODEL // CONV_LANES):
        cs = slice(cblk * CONV_LANES, (cblk + 1) * CONV_LANES)
        gs = slice(D_MODEL + cblk * CONV_LANES, D_MODEL + (cblk + 1) * CONV_LANES)
        ua = _dot(h, w1_ref[:, cs]) + b1_ref[:, cs]
        ub = _dot(h, w1_ref[:, gs]) + b1_ref[:, gs]
        ua_half = 0.5 * ua
        ubuf[CONV_HALO:CONV_HALO + ts, cs] = ua_half + ua_half * jnp.tanh(0.5 * ub)
        window = ubuf[:, cs]
        for sft in range(1, SUBLANES):
            rolled = pltpu.roll(window, CONV_HALO + ts - sft, axis=0)
            shifted[sft - 1, 0:n_sh, cs] = rolled[0:n_sh, :]
        for sub in range(CONV_LANES // LANES):
            ls = slice(cblk * CONV_LANES + sub * LANES, cblk * CONV_LANES + (sub + 1) * LANES)
            for rb in range(ts // CONV_ROWS):
                acc = jnp.broadcast_to(bdw_ref[:, ls], (CONV_ROWS, LANES))
                for k in range(CONV_KERNEL):
                    lo = rb * CONV_ROWS + first + k
                    sft = lo % SUBLANES
                    if sft:
                        tap = shifted[sft - 1, lo - sft:lo - sft + CONV_ROWS, ls]
                    else:
                        tap = ubuf[lo:lo + CONV_ROWS, ls]
                    acc = acc + wdw_ref[k:k + 1, ls] * tap
                dbuf[rb * CONV_ROWS:(rb + 1) * CONV_ROWS, ls] = acc

    for rt in range(ts // CONV_LN_ROWS):
        rows = slice(rt * CONV_LN_ROWS, (rt + 1) * CONV_LN_ROWS)
        acc = dbuf[rows, :]
        mu = jnp.mean(acc, axis=-1, keepdims=True)
        cen = acc - mu
        var = jnp.mean(cen * cen, axis=-1, keepdims=True)
        y = cen * lax.rsqrt(var + EPS) * lng_ref[...] + lnb_ref[...]
        v = _silu(y).astype(BF16)
        out = _dot(v, w2_ref[...]) + b2_ref[...]
        o_ref[rows, :] = x[rows, :] + _rms(out, gpost_ref[...])


def _conv_layer(x3, gpre, gpost, w1_all, b1, wdw, bdw, lng, lnb, w2_all, b2, layer):
    b, s, _ = x3.shape
    ts = CONV_TOKENS
    tok = pl.BlockSpec((None, ts, D_MODEL), lambda bi, j: (bi, j, 0))
    vec = _const_spec((1, D_MODEL))
    return pl.pallas_call(
        _conv_body,
        name="conv_mixer",
        grid=(b, s // ts),
        in_specs=[tok, vec, vec, _layer_spec((D_MODEL, 2 * D_MODEL), layer),
                  _const_spec((1, 2 * D_MODEL)), _const_spec((CONV_KERNEL, D_MODEL)), vec,
                  vec, vec, _layer_spec((D_MODEL, D_MODEL), layer), vec],
        out_specs=tok,
        out_shape=jax.ShapeDtypeStruct(x3.shape, F32),
        scratch_shapes=[pltpu.VMEM((CONV_HALO + ts, D_MODEL), F32),
                        pltpu.VMEM((SUBLANES - 1, CONV_HALO + ts, D_MODEL), F32),
                        pltpu.VMEM((ts, D_MODEL), F32)],
        compiler_params=_params(2, parallel=False),
    )(x3, gpre, gpost, w1_all, b1, wdw, bdw, lng, lnb, w2_all, b2)


def _ssd_body(x_ref, gpre_ref, gpost_ref, win_ref, wdt_ref, cw_ref, cb_ref, dtb_ref,
              expand_ref, alogh_ref, alog_ref, dskip_ref, nw_ref, wout_ref, o_ref, hbuf, xbuf,
              dtbuf, zbuf, ybuf, state):
    ts = x_ref.shape[0]
    lc = SSD_CHUNK
    gw = SSM_D_INNER // SSM_GROUPS
    j = pl.program_id(1)

    @pl.when(j == 0)
    def _():
        hbuf[...] = jnp.zeros(hbuf.shape, F32)
        state[...] = jnp.zeros(state.shape, F32)

    x = x_ref[...]
    h = _rms(x, gpre_ref[...]).astype(BF16)
    dt_raw = _dot(h, wdt_ref[...]) + dtb_ref[...]
    dt_heads = jnp.maximum(dt_raw, 0.0) + jnp.log1p(jnp.exp(-jnp.abs(dt_raw)))
    dtbuf[...] = _dot_exact01_rhs(dt_heads, expand_ref[...])

    n_cblk = SSM_CONV_DIM // SSD_CONV_LANES

    def project(cblk):
        ws = slice(SSM_D_INNER + cblk * SSD_CONV_LANES, SSM_D_INNER + (cblk + 1) * SSD_CONV_LANES)
        return _dot(h, win_ref[:, ws])

    raw = project(0)
    for cblk in range(n_cblk):
        cs = slice(cblk * SSD_CONV_LANES, (cblk + 1) * SSD_CONV_LANES)
        if cblk + 1 < n_cblk:
            raw_next = project(cblk + 1)
        else:
            raw_next = None
            zbuf[...] = _dot(h, win_ref[:, 0:SSM_D_INNER])
        window = jnp.concatenate([hbuf[:, cs], raw], axis=0)
        hbuf[:, cs] = raw[ts - SSD_HALO:ts, :]
        raw = raw_next
        conv = jnp.broadcast_to(cb_ref[:, cs], (ts, SSD_CONV_LANES))
        for k in range(SSM_CONV):
            back = SSM_CONV - 1 - k
            tap = pltpu.roll(window, back, axis=0) if back else window
            conv = conv + cw_ref[k:k + 1, cs] * tap[SSD_HALO:SSD_HALO + ts, :]
        xbuf[:, cs] = _silu(conv)

    a_row = -jnp.exp(alog_ref[...]) * LOG2E
    a_heads = -jnp.exp(alogh_ref[...]) * LOG2E
    ri = lax.broadcasted_iota(jnp.int32, (lc, lc), 0)
    ci = lax.broadcasted_iota(jnp.int32, (lc, lc), 1)
    causal = ci <= ri
    tril01 = causal.astype(BF16)
    off_b = SSM_D_INNER
    off_c = SSM_D_INNER + SSM_GROUPS * SSM_STATE
    lane = lax.broadcasted_iota(jnp.int32, (1, gw), 1)
    head01 = [((lane >= r * SSM_HEAD_DIM) & (lane < (r + 1) * SSM_HEAD_DIM)).astype(BF16)
              for r in range(gw // SSM_HEAD_DIM)]

    def chunk_terms(c):
        rows = slice(c * lc, (c + 1) * lc)
        dt = dtbuf[rows, :]
        a_cs = _dot_exact01(tril01, dt * a_row)
        a_last = a_cs[lc - 1:lc, :]
        a_csh = _dot_exact01(tril01, dt_heads[rows, :] * a_heads)
        return dict(rows=rows, dt=dt, exp_a=jnp.exp2(a_cs), to_end=jnp.exp2(a_last - a_cs),
                    decay_chunk=jnp.exp2(a_last), a_csh=a_csh, a_csh_t=a_csh.T)

    def group_inputs(ck, g):
        rows = ck["rows"]
        gl = slice(g * gw, (g + 1) * gw)
        xs = xbuf[rows, gl]
        xdt = xs * ck["dt"][:, gl]
        bm = xbuf[rows, off_b + g * SSM_STATE:off_b + (g + 1) * SSM_STATE].astype(BF16)
        cm = xbuf[rows, off_c + g * SSM_STATE:off_c + (g + 1) * SSM_STATE].astype(BF16)
        return xs, xdt, bm, cm, _dot_nt(cm, bm)

    def group_outputs(ck, g, xs, xdt, bm, cm, cb):
        rows = ck["rows"]
        gl = slice(g * gw, (g + 1) * gw)
        xdb = xdt.astype(BF16)
        ms, xds = [], []
        for r in range(gw // SSM_HEAD_DIM):
            hd = g * (gw // SSM_HEAD_DIM) + r
            col = ck["a_csh"][:, hd:hd + 1]
            row = ck["a_csh_t"][hd:hd + 1, :]
            decay = jnp.where(causal, jnp.exp2(col - row), 0.0)
            ms.append((cb * decay).astype(BF16))
            xds.append(xdb * head01[r])
        y = (_dot(jnp.concatenate(ms[0:2], axis=1), jnp.concatenate(xds[0:2], axis=0))
             + _dot(jnp.concatenate(ms[2:4], axis=1), jnp.concatenate(xds[2:4], axis=0)))
        st = state[g]
        y = y + _dot(cm, st.astype(BF16)) * ck["exp_a"][:, gl]
        xw = (xdt * ck["to_end"][:, gl]).astype(BF16)
        state[g] = st * ck["decay_chunk"][:, gl] + _dot_tn(bm, xw)
        y = y + dskip_ref[:, gl] * xs
        y = y * _silu(zbuf[rows, gl])
        y = y * lax.rsqrt(jnp.mean(y * y, axis=-1, keepdims=True) + EPS) * nw_ref[:, gl]
        ybuf[rows, gl] = y.astype(BF16)

    units = [(c, g) for c in range(ts // lc) for g in range(SSM_GROUPS)]
    chunks = {}

    def issue(n):
        c1, g1 = units[n]
        if c1 not in chunks:
            chunks[c1] = chunk_terms(c1)
        return group_inputs(chunks[c1], g1)

    pending = [issue(n) for n in range(SSD_AHEAD)]
    for n, (c, g) in enumerate(units):
        current = pending.pop(0)
        if n + SSD_AHEAD < len(units):
            pending.append(issue(n + SSD_AHEAD))
        group_outputs(chunks[c], g, *current)
        if g == SSM_GROUPS - 1:
            rows = chunks[c]["rows"]
            out = _dot(ybuf[rows, :], wout_ref[...])
            o_ref[rows, :] = x[rows, :] + _rms(out, gpost_ref[...])


def _ssd_layer(x3, gpre, gpost, win, wdt, cw, cb, dtb, expand, alogh, alog, dskip, nw, wout):
    b, s, _ = x3.shape
    ts = SSD_TOKENS
    tok = pl.BlockSpec((None, ts, D_MODEL), lambda bi, j: (bi, j, 0))
    vec = _const_spec((1, D_MODEL))
    inner = _const_spec((1, SSM_D_INNER))
    return pl.pallas_call(
        _ssd_body,
        name="ssd_mixer",
        grid=(b, s // ts),
        in_specs=[tok, vec, vec, _const_spec(win.shape), _const_spec((D_MODEL, LANES)),
                  _const_spec((SSM_CONV, SSM_CONV_DIM)), _const_spec((1, SSM_CONV_DIM)),
                  _const_spec((1, LANES)), _const_spec((LANES, SSM_D_INNER)), _const_spec((1, LANES)),
                  inner, inner, inner, _const_spec((SSM_D_INNER, D_MODEL))],
        out_specs=tok,
        out_shape=jax.ShapeDtypeStruct(x3.shape, F32),
        scratch_shapes=[pltpu.VMEM((SSD_HALO, SSM_CONV_DIM), F32),
                        pltpu.VMEM((ts, SSM_CONV_DIM), F32),
                        pltpu.VMEM((ts, SSM_D_INNER), F32),
                        pltpu.VMEM((ts, SSM_D_INNER), F32),
                        pltpu.VMEM((ts, SSM_D_INNER), BF16),
                        pltpu.VMEM((SSM_GROUPS, SSM_STATE, SSM_D_INNER // SSM_GROUPS), F32)],
        compiler_params=_params(2, parallel=False),
    )(x3, gpre, gpost, win, wdt, cw, cb, dtb, expand, alogh, alog, dskip, nw, wout)


def _mla_proj_body(x_ref, pos_ref, gpre_ref, wlat_ref, qn_ref, kvn_ref, wq_ref, wqr_ref, wk_ref,
                   wv_ref, inv_ref, q_ref, k_ref, v_ref):
    o1 = MLA_Q_LORA
    o2 = o1 + MLA_KV_LORA
    reps = (1, MLA_HEADS)

    def latent(sub):
        rows = slice(sub * MLA_SUB, (sub + 1) * MLA_SUB)
        h = _rms(x_ref[rows, :], gpre_ref[...]).astype(BF16)
        return _dot(h, wlat_ref[...])

    n_sub = x_ref.shape[0] // MLA_SUB
    lat_next = latent(0)
    for sub in range(n_sub):
        rows = slice(sub * MLA_SUB, (sub + 1) * MLA_SUB)
        lat = lat_next
        if sub + 1 < n_sub:
            lat_next = latent(sub + 1)
        cq = _rms(lat[:, :o1], qn_ref[...]).astype(BF16)
        ckv = _rms(lat[:, o1:o2], kvn_ref[...]).astype(BF16)
        pe = lat[:, o2:o2 + ATT_PAD]
        pe_rot = lat[:, o2 + ATT_PAD:o2 + 2 * ATT_PAD]
        ang = pos_ref[rows, :].astype(F32) * inv_ref[...]
        cos = jnp.cos(ang)
        sin = jnp.sin(ang)
        k_pe = pe * cos + pe_rot * sin
        k_ref[rows, :] = (_dot(ckv, wk_ref[...]) + jnp.tile(k_pe, reps)).astype(BF16)
        q = (_dot(cq, wq_ref[...]) * jnp.tile(cos, reps)
             + _dot(cq, wqr_ref[...]) * jnp.tile(sin, reps))
        q_ref[rows, :] = q.astype(BF16)
        v_ref[rows, :] = _dot(ckv, wv_ref[...]).astype(BF16)


def _mla_proj(x2, pos2, gpre, wlat, qn, kvn, wq, wqr, wk, wv, inv):
    t = x2.shape[0]
    tm = MLA_TOKENS
    hp = MLA_HEADS * ATT_PAD
    hv = MLA_HEADS * MLA_V

    def tok(w):
        return pl.BlockSpec((tm, w), lambda i: (i, 0))

    return pl.pallas_call(
        _mla_proj_body,
        name="mla_proj",
        grid=(t // tm,),
        in_specs=[tok(D_MODEL), tok(1), _const_spec((1, D_MODEL)), _const_spec(wlat.shape),
                  _const_spec((1, MLA_Q_LORA)), _const_spec((1, MLA_KV_LORA)),
                  _const_spec((MLA_Q_LORA, hp)), _const_spec((MLA_Q_LORA, hp)),
                  _const_spec((MLA_KV_LORA, hp)), _const_spec((MLA_KV_LORA, hv)),
                  _const_spec((1, ATT_PAD))],
        out_specs=[tok(hp), tok(hp), tok(hv)],
        out_shape=[jax.ShapeDtypeStruct((t, hp), BF16), jax.ShapeDtypeStruct((t, hp), BF16),
                   jax.ShapeDtypeStruct((t, hv), BF16)],
        compiler_params=_params(1),
    )(x2, pos2, gpre, wlat, qn, kvn, wq, wqr, wk, wv, inv)


def _attn_body(q_ref, k_ref, v_ref, o_ref, s_buf):
    s_len = q_ref.shape[0]
    tq = ATT_Q
    c2 = (MLA_NOPE + MLA_ROPE) ** -0.5 * math.log2(math.e)
    ri = lax.broadcasted_iota(jnp.int32, (tq, tq), 0) // CHUNK
    ci = lax.broadcasted_iota(jnp.int32, (tq, tq), 1) // CHUNK
    diag_mask = ci <= ri
    lane = lax.broadcasted_iota(jnp.int32, (tq, 2 * MLA_V), 1)

    def scores(i, e):
        lo = i * tq
        qrows = slice(lo, lo + tq)
        cols = slice(e * ATT_PAD, (e + 1) * ATT_PAD)
        sb = s_buf.at[2 * (i % 2) + e]
        q = q_ref[qrows, cols]
        sd = jnp.where(diag_mask, _dot_nt(q, k_ref[qrows, cols]), -jnp.inf)
        m = jnp.max(sd, axis=-1, keepdims=True)
        if lo:
            sb[:, 0:lo] = _dot_nt(q, k_ref[0:lo, cols])
            m = jnp.maximum(m, jnp.max(sb[:, 0:lo], axis=-1, keepdims=True))
        return sd, m

    def weighted_values(i, e, sd, m):
        lo = i * tq
        qrows = slice(lo, lo + tq)
        sb = s_buf.at[2 * (i % 2) + e]
        pd = jnp.exp2((sd - m) * c2)
        l = jnp.sum(pd, axis=-1, keepdims=True)
        acc = _dot(pd.astype(BF16), v_ref[qrows, :])
        if lo:
            po = jnp.exp2((sb[:, 0:lo] - m) * c2)
            l = l + jnp.sum(po, axis=-1, keepdims=True)
            acc = acc + _dot(po.astype(BF16), v_ref[0:lo, :])
        return acc / l

    units = [(i, e) for i in range(s_len // tq) for e in range(2)]
    pending = [scores(*u) for u in units[:ATT_AHEAD]]
    outs = {}
    for n, (i, e) in enumerate(units):
        sd, m = pending.pop(0)
        if n + ATT_AHEAD < len(units):
            pending.append(scores(*units[n + ATT_AHEAD]))
        outs[e] = weighted_values(i, e, sd, m)
        if e == 1:
            qrows = slice(i * tq, (i + 1) * tq)
            o_ref[qrows, :] = jnp.where(lane < MLA_V, outs[0], outs[1]).astype(BF16)


def _attention(q3, k3, v3):
    b, s, _ = q3.shape
    return pl.pallas_call(
        _attn_body,
        name="mla_attn",
        grid=(b, MLA_HEADS // 2),
        in_specs=[pl.BlockSpec((None, s, 2 * ATT_PAD), lambda bi, hp: (bi, 0, hp)),
                  pl.BlockSpec((None, s, 2 * ATT_PAD), lambda bi, hp: (bi, 0, hp)),
                  pl.BlockSpec((None, s, 2 * MLA_V), lambda bi, hp: (bi, 0, hp))],
        out_specs=pl.BlockSpec((None, s, 2 * MLA_V), lambda bi, hp: (bi, 0, hp)),
        out_shape=jax.ShapeDtypeStruct((b, s, MLA_HEADS * MLA_V), BF16),
        scratch_shapes=[pltpu.VMEM((4, ATT_Q, s - ATT_Q), F32)],
        compiler_params=_params(2),
    )(q3, k3, v3)


def _out_proj_body(y_ref, x_ref, w_ref, gpost_ref, o_ref):
    o_ref[...] = x_ref[...] + _rms(_dot(y_ref[...], w_ref[...]), gpost_ref[...])


def _out_proj(y2, x2, w, gpost):
    t, kdim = y2.shape
    tm = OUT_TOKENS
    return pl.pallas_call(
        _out_proj_body,
        name="out_proj",
        grid=(t // tm,),
        in_specs=[pl.BlockSpec((tm, kdim), lambda i: (i, 0)),
                  pl.BlockSpec((tm, D_MODEL), lambda i: (i, 0)),
                  _const_spec((kdim, D_MODEL)), _const_spec((1, D_MODEL))],
        out_specs=pl.BlockSpec((tm, D_MODEL), lambda i: (i, 0)),
        out_shape=jax.ShapeDtypeStruct((t, D_MODEL), F32),
        compiler_params=_params(1),
    )(y2, x2, w, gpost)


def _row(v):
    return v.reshape(1, -1).astype(F32)


def _pad_heads(w, width, offset=0):
    k, hh, d = w.shape
    out = jnp.zeros((k, hh, width), w.dtype)
    out = out.at[:, :, offset:offset + d].set(w)
    return out.reshape(k, hh * width)


def _rot_half_cols(w):
    half = w.shape[-1] // 2
    return jnp.concatenate([-w[..., half:], w[..., :half]], axis=-1)


def _mla_layer(x2, pos2, b, s, gpre, gpost, w_in, q_norm, w_uq, kv_norm, w_ukv, w_o):
    o1 = MLA_Q_LORA
    o2 = o1 + MLA_KV_LORA
    w_pe = w_in[:, o2:]
    slab = jnp.zeros((D_MODEL, ATT_PAD), F32).at[:, MLA_NOPE:MLA_NOPE + MLA_ROPE]
    wlat = jnp.concatenate([w_in[:, :o2], slab.set(w_pe), slab.set(_rot_half_cols(w_pe))],
                           axis=1).astype(BF16)
    wq3 = w_uq.reshape(o1, MLA_HEADS, MLA_NOPE + MLA_ROPE)
    wq = _pad_heads(wq3, ATT_PAD).astype(BF16)
    wqr = _pad_heads(_rot_half_cols(wq3[:, :, MLA_NOPE:]), ATT_PAD, MLA_NOPE).astype(BF16)
    wkv3 = w_ukv.reshape(MLA_KV_LORA, MLA_HEADS, MLA_NOPE + MLA_V)
    wk = _pad_heads(wkv3[:, :, :MLA_NOPE], ATT_PAD).astype(BF16)
    wv = wkv3[:, :, MLA_NOPE:].reshape(MLA_KV_LORA, MLA_HEADS * MLA_V).astype(BF16)
    half = MLA_ROPE // 2
    inv = ROPE_BASE ** (-jnp.arange(half, dtype=F32) / half)
    inv_slab = jnp.zeros((1, ATT_PAD), F32).at[0, MLA_NOPE:MLA_NOPE + MLA_ROPE].set(
        jnp.concatenate([inv, inv]))
    q, k, v = _mla_proj(x2, pos2, _row(gpre), wlat, _row(q_norm), _row(kv_norm), wq, wqr, wk, wv,
                        inv_slab)
    o = _attention(q.reshape(b, s, -1), k.reshape(b, s, -1), v.reshape(b, s, -1))
    return _out_proj(o.reshape(b * s, -1), x2, w_o.astype(BF16), _row(gpost))


def kernel(x, positions, norm_mix_pre, norm_mix_post, norm_ffn_pre, norm_ffn_post, ffn_w_in, ffn_w_out, conv_w_pw1, conv_b_pw1, conv_w_dw, conv_b_dw, conv_ln_g, conv_ln_b, conv_w_pw2, conv_b_pw2, ssm_w_in, ssm_conv_w, ssm_conv_b, ssm_dt_bias, ssm_a_log, ssm_d, ssm_norm_w, ssm_w_out, mla_w_in, mla_q_norm, mla_w_uq, mla_kv_norm, mla_w_ukv, mla_w_o):
    b, s, d = x.shape
    t = b * s
    pos2 = positions.reshape(t, 1)
    ffn_w1 = ffn_w_in.astype(BF16)
    ffn_w2 = ffn_w_out.astype(BF16)
    conv_w1 = conv_w_pw1.astype(BF16)
    conv_w2 = conv_w_pw2.astype(BF16)
    i_conv = i_ssm = i_mla = 0
    for i in range(DEPTH):
        kind = i % N_MIXERS
        gpre, gpost = _row(norm_mix_pre[i]), _row(norm_mix_post[i])
        if kind == 0:
            jx = i_conv
            x = _conv_layer(
                x.reshape(b, s, d), gpre, gpost, conv_w1, _row(conv_b_pw1[jx]),
                conv_w_dw[jx], _row(conv_b_dw[jx]), _row(conv_ln_g[jx]), _row(conv_ln_b[jx]),
                conv_w2, _row(conv_b_pw2[jx]), jx)
            i_conv += 1
        elif kind == 1:
            jx = i_ssm
            w_in = ssm_w_in[jx].astype(BF16)
            o2 = SSM_D_INNER + SSM_CONV_DIM
            rep = SSM_HEAD_DIM
            pad = LANES - SSM_HEADS
            expand = jnp.repeat(jnp.eye(LANES, SSM_HEADS, dtype=BF16), rep, axis=1)
            x = _ssd_layer(
                x.reshape(b, s, d), gpre, gpost, w_in, jnp.pad(w_in[:, o2:], ((0, 0), (0, pad))),
                ssm_conv_w[jx], _row(ssm_conv_b[jx]), _row(jnp.pad(ssm_dt_bias[jx], (0, pad))),
                expand, _row(jnp.pad(ssm_a_log[jx], (0, pad))),
                _row(jnp.repeat(ssm_a_log[jx], rep)), _row(jnp.repeat(ssm_d[jx], rep)),
                _row(ssm_norm_w[jx]), ssm_w_out[jx].astype(BF16))
            i_ssm += 1
        else:
            jx = i_mla
            x = _mla_layer(x.reshape(t, d), pos2, b, s, norm_mix_pre[i], norm_mix_post[i],
                           mla_w_in[jx], mla_q_norm[jx], mla_w_uq[jx], mla_kv_norm[jx],
                           mla_w_ukv[jx], mla_w_o[jx])
            i_mla += 1
        x = _ffn(x.reshape(t, d), _row(norm_ffn_pre[i]), _row(norm_ffn_post[i]), ffn_w1, ffn_w2, i)
    return x.reshape(b, s, d)
```

```python
import math

import jax
import jax.numpy as jnp
from jax import lax
from jax.experimental import pallas as pl
from jax.experimental.pallas import tpu as pltpu

F32 = jnp.float32
BF16 = jnp.bfloat16

D_MODEL = 1024
DEPTH = 4
CHUNK = 64
N_MIXERS = 3
EPS = 1e-6
FFN_DIM = 4 * D_MODEL
CONV_KERNEL = 31
SSM_D_INNER = 2 * D_MODEL
SSM_HEAD_DIM = 64
SSM_HEADS = SSM_D_INNER // SSM_HEAD_DIM
SSM_GROUPS = 8
SSM_STATE = 128
SSM_CONV = 4
SSM_CONV_DIM = SSM_D_INNER + 2 * SSM_GROUPS * SSM_STATE
MLA_HEADS = D_MODEL // 64
MLA_NOPE = 64
MLA_ROPE = 32
MLA_V = 64
MLA_Q_LORA = 3 * D_MODEL // 8
MLA_KV_LORA = D_MODEL // 4
ROPE_BASE = 10000.0
LOG2E = math.log2(math.e)

LANES = 128
SUBLANES = 8
VMEM_LIMIT_BYTES = 56 * 1024 * 1024

FFN_TOKENS = 1024
FFN_SUB = 512
FFN_CHUNK = 512
CONV_TOKENS = 512
CONV_HALO = 32
CONV_ROWS = 32
CONV_LN_ROWS = 128
CONV_LANES = 256
SSD_TOKENS = 512
SSD_CHUNK = 128
SSD_HALO = SUBLANES
SSD_CONV_LANES = 512
SSD_AHEAD = 1
MLA_TOKENS = 512
MLA_SUB = 256
ATT_Q = 256
ATT_PAD = 128
ATT_AHEAD = 2
OUT_TOKENS = 1024


def _const_spec(shape):
    nd = len(shape)
    return pl.BlockSpec(shape, lambda *_: (0,) * nd, pipeline_mode=pl.Buffered(1))


def _layer_spec(shape, layer):
    nd = len(shape)
    return pl.BlockSpec((None,) + tuple(shape), lambda *_: (layer,) + (0,) * nd,
                        pipeline_mode=pl.Buffered(1))


def _params(n_axes, parallel=True):
    sem = ("parallel" if parallel else "arbitrary",) * n_axes
    return pltpu.CompilerParams(dimension_semantics=sem, vmem_limit_bytes=VMEM_LIMIT_BYTES)


def _rms(x, g):
    return x * lax.rsqrt(jnp.mean(x * x, axis=-1, keepdims=True) + EPS) * g


def _silu(x):
    half = 0.5 * x
    return half + half * jnp.tanh(half)


def _dot(a, b):
    return jnp.dot(a, b, preferred_element_type=F32)


def _dot_nt(a, b):
    return lax.dot_general(a, b, (((1,), (1,)), ((), ())), preferred_element_type=F32)


def _dot_tn(a, b):
    return lax.dot_general(a, b, (((0,), (0,)), ((), ())), preferred_element_type=F32)


def _split3(x):
    hi = x.astype(BF16)
    r1 = x - hi.astype(F32)
    mid = r1.astype(BF16)
    lo = (r1 - mid.astype(F32)).astype(BF16)
    return hi, mid, lo


def _dot_exact01(m01, x):
    hi, mid, lo = _split3(x)
    return (_dot(jnp.concatenate([m01, m01], axis=1), jnp.concatenate([hi, mid], axis=0))
            + _dot(m01, lo))


def _dot_exact01_rhs(x, m01):
    hi, mid, lo = _split3(x)
    return (_dot(jnp.concatenate([hi, mid], axis=1), jnp.concatenate([m01, m01], axis=0))
            + _dot(lo, m01))


def _ffn_body(x_ref, gpre_ref, gpost_ref, w1_ref, w2_ref, o_ref):
    for sub in range(FFN_TOKENS // FFN_SUB):
        rows = slice(sub * FFN_SUB, (sub + 1) * FFN_SUB)
        x = x_ref[rows, :]
        h = _rms(x, gpre_ref[...]).astype(BF16)
        acc = None
        for c in range(FFN_DIM // FFN_CHUNK):
            cols = slice(c * FFN_CHUNK, (c + 1) * FFN_CHUNK)
            a = _dot(h, w1_ref[:, cols])
            a = jnp.square(jnp.maximum(a, 0.0)).astype(BF16)
            p = _dot(a, w2_ref[cols, :])
            acc = p if acc is None else acc + p
        o_ref[rows, :] = x + _rms(acc, gpost_ref[...])


def _ffn(x2, gpre, gpost, w1_all, w2_all, layer):
    t = x2.shape[0]
    tok = pl.BlockSpec((FFN_TOKENS, D_MODEL), lambda i: (i, 0))
    return pl.pallas_call(
        _ffn_body,
        name="ffn",
        grid=(t // FFN_TOKENS,),
        in_specs=[tok, _const_spec((1, D_MODEL)), _const_spec((1, D_MODEL)),
                  _layer_spec((D_MODEL, FFN_DIM), layer), _layer_spec((FFN_DIM, D_MODEL), layer)],
        out_specs=tok,
        out_shape=jax.ShapeDtypeStruct((t, D_MODEL), F32),
        compiler_params=_params(1),
    )(x2, gpre, gpost, w1_all, w2_all)


def _conv_body(x_ref, gpre_ref, gpost_ref, w1_ref, b1_ref, wdw_ref, bdw_ref,
               lng_ref, lnb_ref, w2_ref, b2_ref, o_ref, ubuf, shifted, dbuf):
    ts = x_ref.shape[0]
    j = pl.program_id(1)

    @pl.when(j == 0)
    def _():
        ubuf[0:CONV_HALO, :] = jnp.zeros((CONV_HALO, D_MODEL), F32)

    @pl.when(j > 0)
    def _():
        ubuf[0:CONV_HALO, :] = ubuf[ts:ts + CONV_HALO, :]

    x = x_ref[...]
    h = _rms(x, gpre_ref[...]).astype(BF16)
    first = CONV_HALO - (CONV_KERNEL - 1)
    n_sh = CONV_HALO + ts - SUBLANES

    for cblk in range(D_MODEL // CONV_LANES):
        cs = slice(cblk * CONV_LANES, (cblk + 1) * CONV_LANES)
        gs = slice(D_MODEL + cblk * CONV_LANES, D_MODEL + (cblk + 1) * CONV_LANES)
        ua = _dot(h, w1_ref[:, cs]) + b1_ref[:, cs]
        ub = _dot(h, w1_ref[:, gs]) + b1_ref[:, gs]
        ua_half = 0.5 * ua
        ubuf[CONV_HALO:CONV_HALO + ts, cs] = ua_half + ua_half * jnp.tanh(0.5 * ub)
        window = ubuf[:, cs]
        for sft in range(1, SUBLANES):
            rolled = pltpu.roll(window, CONV_HALO + ts - sft, axis=0)
            shifted[sft - 1, 0:n_sh, cs] = rolled[0:n_sh, :]
        for sub in range(CONV_LANES // LANES):
            ls = slice(cblk * CONV_LANES + sub * LANES, cblk * CONV_LANES + (sub + 1) * LANES)
            for rb in range(ts // CONV_ROWS):
                acc = jnp.broadcast_to(bdw_ref[:, ls], (CONV_ROWS, LANES))
                for k in range(CONV_KERNEL):
                    lo = rb * CONV_ROWS + first + k
                    sft = lo % SUBLANES
                    if sft:
                        tap = shifted[sft - 1, lo - sft:lo - sft + CONV_ROWS, ls]
                    else:
                        tap = ubuf[lo:lo + CONV_ROWS, ls]
                    acc = acc + wdw_ref[k:k + 1, ls] * tap
                dbuf[rb * CONV_ROWS:(rb + 1) * CONV_ROWS, ls] = acc

    for rt in range(ts // CONV_LN_ROWS):
        rows = slice(rt * CONV_LN_ROWS, (rt + 1) * CONV_LN_ROWS)
        acc = dbuf[rows, :]
        mu = jnp.mean(acc, axis=-1, keepdims=True)
        cen = acc - mu
        var = jnp.mean(cen * cen, axis=-1, keepdims=True)
        y = cen * lax.rsqrt(var + EPS) * lng_ref[...] + lnb_ref[...]
        v = _silu(y).astype(BF16)
        out = _dot(v, w2_ref[...]) + b2_ref[...]
        o_ref[rows, :] = x[rows, :] + _rms(out, gpost_ref[...])


def _conv_layer(x3, gpre, gpost, w1_all, b1, wdw, bdw, lng, lnb, w2_all, b2, layer):
    b, s, _ = x3.shape
    ts = CONV_TOKENS
    tok = pl.BlockSpec((None, ts, D_MODEL), lambda bi, j: (bi, j, 0))
    vec = _const_spec((1, D_MODEL))
    return pl.pallas_call(
        _conv_body,
        name="conv_mixer",
        grid=(b, s // ts),
        in_specs=[tok, vec, vec, _layer_spec((D_MODEL, 2 * D_MODEL), layer),
                  _const_spec((1, 2 * D_MODEL)), _const_spec((CONV_KERNEL, D_MODEL)), vec,
                  vec, vec, _layer_spec((D_MODEL, D_MODEL), layer), vec],
        out_specs=tok,
        out_shape=jax.ShapeDtypeStruct(x3.shape, F32),
        scratch_shapes=[pltpu.VMEM((CONV_HALO + ts, D_MODEL), F32),
                        pltpu.VMEM((SUBLANES - 1, CONV_HALO + ts, D_MODEL), F32),
                        pltpu.VMEM((ts, D_MODEL), F32)],
        compiler_params=_params(2, parallel=False),
    )(x3, gpre, gpost, w1_all, b1, wdw, bdw, lng, lnb, w2_all, b2)


def _ssd_body(x_ref, gpre_ref, gpost_ref, win_ref, wdt_ref, cw_ref, cb_ref, dtb_ref,
              expand_ref, alogh_ref, alog_ref, dskip_ref, nw_ref, wout_ref, o_ref, hbuf, xbuf,
              dtbuf, zbuf, ybuf, state):
    ts = x_ref.shape[0]
    lc = SSD_CHUNK
    gw = SSM_D_INNER // SSM_GROUPS
    j = pl.program_id(1)

    @pl.when(j == 0)
    def _():
        hbuf[...] = jnp.zeros(hbuf.shape, F32)
        state[...] = jnp.zeros(state.shape, F32)

    x = x_ref[...]
    h = _rms(x, gpre_ref[...]).astype(BF16)
    dt_raw = _dot(h, wdt_ref[...]) + dtb_ref[...]
    dt_heads = jnp.maximum(dt_raw, 0.0) + jnp.log1p(jnp.exp(-jnp.abs(dt_raw)))
    dtbuf[...] = _dot_exact01_rhs(dt_heads, expand_ref[...])

    n_cblk = SSM_CONV_DIM // SSD_CONV_LANES

    def project(cblk):
        ws = slice(SSM_D_INNER + cblk * SSD_CONV_LANES, SSM_D_INNER + (cblk + 1) * SSD_CONV_LANES)
        return _dot(h, win_ref[:, ws])

    raw = project(0)
    for cblk in range(n_cblk):
        cs = slice(cblk * SSD_CONV_LANES, (cblk + 1) * SSD_CONV_LANES)
        if cblk + 1 < n_cblk:
            raw_next = project(cblk + 1)
        else:
            raw_next = None
            zbuf[...] = _dot(h, win_ref[:, 0:SSM_D_INNER])
        window = jnp.concatenate([hbuf[:, cs], raw], axis=0)
        hbuf[:, cs] = raw[ts - SSD_HALO:ts, :]
        raw = raw_next
        conv = jnp.broadcast_to(cb_ref[:, cs], (ts, SSD_CONV_LANES))
        for k in range(SSM_CONV):
            back = SSM_CONV - 1 - k
            tap = pltpu.roll(window, back, axis=0) if back else window
            conv = conv + cw_ref[k:k + 1, cs] * tap[SSD_HALO:SSD_HALO + ts, :]
        xbuf[:, cs] = _silu(conv)

    a_row = -jnp.exp(alog_ref[...]) * LOG2E
    a_heads = -jnp.exp(alogh_ref[...]) * LOG2E
    ri = lax.broadcasted_iota(jnp.int32, (lc, lc), 0)
    ci = lax.broadcasted_iota(jnp.int32, (lc, lc), 1)
    causal = ci <= ri
    tril01 = causal.astype(BF16)
    off_b = SSM_D_INNER
    off_c = SSM_D_INNER + SSM_GROUPS * SSM_STATE
    lane = lax.broadcasted_iota(jnp.int32, (1, gw), 1)
    head01 = [((lane >= r * SSM_HEAD_DIM) & (lane < (r + 1) * SSM_HEAD_DIM)).astype(BF16)
              for r in range(gw // SSM_HEAD_DIM)]

    def chunk_terms(c):
        rows = slice(c * lc, (c + 1) * lc)
        dt = dtbuf[rows, :]
        a_cs = _dot_exact01(tril01, dt * a_row)
        a_last = a_cs[lc - 1:lc, :]
        a_csh = _dot_exact01(tril01, dt_heads[rows, :] * a_heads)
        return dict(rows=rows, dt=dt, exp_a=jnp.exp2(a_cs), to_end=jnp.exp2(a_last - a_cs),
                    decay_chunk=jnp.exp2(a_last), a_csh=a_csh, a_csh_t=a_csh.T)

    def group_inputs(ck, g):
        rows = ck["rows"]
        gl = slice(g * gw, (g + 1) * gw)
        xs = xbuf[rows, gl]
        xdt = xs * ck["dt"][:, gl]
        bm = xbuf[rows, off_b + g * SSM_STATE:off_b + (g + 1) * SSM_STATE].astype(BF16)
        cm = xbuf[rows, off_c + g * SSM_STATE:off_c + (g + 1) * SSM_STATE].astype(BF16)
        return xs, xdt, bm, cm, _dot_nt(cm, bm)

    def group_outputs(ck, g, xs, xdt, bm, cm, cb):
        rows = ck["rows"]
        gl = slice(g * gw, (g + 1) * gw)
        xdb = xdt.astype(BF16)
        ms, xds = [], []
        for r in range(gw // SSM_HEAD_DIM):
            hd = g * (gw // SSM_HEAD_DIM) + r
            col = ck["a_csh"][:, hd:hd + 1]
            row = ck["a_csh_t"][hd:hd + 1, :]
            decay = jnp.where(causal, jnp.exp2(col - row), 0.0)
            ms.append((cb * decay).astype(BF16))
            xds.append(xdb * head01[r])
        y = (_dot(jnp.concatenate(ms[0:2], axis=1), jnp.concatenate(xds[0:2], axis=0))
             + _dot(jnp.concatenate(ms[2:4], axis=1), jnp.concatenate(xds[2:4], axis=0)))
        st = state[g]
        y = y + _dot(cm, st.astype(BF16)) * ck["exp_a"][:, gl]
        xw = (xdt * ck["to_end"][:, gl]).astype(BF16)
        state[g] = st * ck["decay_chunk"][:, gl] + _dot_tn(bm, xw)
        y = y + dskip_ref[:, gl] * xs
        y = y * _silu(zbuf[rows, gl])
        y = y * lax.rsqrt(jnp.mean(y * y, axis=-1, keepdims=True) + EPS) * nw_ref[:, gl]
        ybuf[rows, gl] = y.astype(BF16)

    units = [(c, g) for c in range(ts // lc) for g in range(SSM_GROUPS)]
    chunks = {}

    def issue(n):
        c1, g1 = units[n]
        if c1 not in chunks:
            chunks[c1] = chunk_terms(c1)
        return group_inputs(chunks[c1], g1)

    pending = [issue(n) for n in range(SSD_AHEAD)]
    for n, (c, g) in enumerate(units):
        current = pending.pop(0)
        if n + SSD_AHEAD < len(units):
            pending.append(issue(n + SSD_AHEAD))
        group_outputs(chunks[c], g, *current)
        if g == SSM_GROUPS - 1:
            rows = chunks[c]["rows"]
            out = _dot(ybuf[rows, :], wout_ref[...])
            o_ref[rows, :] = x[rows, :] + _rms(out, gpost_ref[...])


def _ssd_layer(x3, gpre, gpost, win, wdt, cw, cb, dtb, expand, alogh, alog, dskip, nw, wout):
    b, s, _ = x3.shape
    ts = SSD_TOKENS
    tok = pl.BlockSpec((None, ts, D_MODEL), lambda bi, j: (bi, j, 0))
    vec = _const_spec((1, D_MODEL))
    inner = _const_spec((1, SSM_D_INNER))
    return pl.pallas_call(
        _ssd_body,
        name="ssd_mixer",
        grid=(b, s // ts),
        in_specs=[tok, vec, vec, _const_spec(win.shape), _const_spec((D_MODEL, LANES)),
                  _const_spec((SSM_CONV, SSM_CONV_DIM)), _const_spec((1, SSM_CONV_DIM)),
                  _const_spec((1, LANES)), _const_spec((LANES, SSM_D_INNER)), _const_spec((1, LANES)),
                  inner, inner, inner, _const_spec((SSM_D_INNER, D_MODEL))],
        out_specs=tok,
        out_shape=jax.ShapeDtypeStruct(x3.shape, F32),
        scratch_shapes=[pltpu.VMEM((SSD_HALO, SSM_CONV_DIM), F32),
                        pltpu.VMEM((ts, SSM_CONV_DIM), F32),
                        pltpu.VMEM((ts, SSM_D_INNER), F32),
                        pltpu.VMEM((ts, SSM_D_INNER), F32),
                        pltpu.VMEM((ts, SSM_D_INNER), BF16),
                        pltpu.VMEM((SSM_GROUPS, SSM_STATE, SSM_D_INNER // SSM_GROUPS), F32)],
        compiler_params=_params(2, parallel=False),
    )(x3, gpre, gpost, win, wdt, cw, cb, dtb, expand, alogh, alog, dskip, nw, wout)


def _mla_proj_body(x_ref, pos_ref, gpre_ref, wlat_ref, qn_ref, kvn_ref, wq_ref, wqr_ref, wk_ref,
                   wv_ref, inv_ref, q_ref, k_ref, v_ref):
    o1 = MLA_Q_LORA
    o2 = o1 + MLA_KV_LORA
    reps = (1, MLA_HEADS)

    def latent(sub):
        rows = slice(sub * MLA_SUB, (sub + 1) * MLA_SUB)
        h = _rms(x_ref[rows, :], gpre_ref[...]).astype(BF16)
        return _dot(h, wlat_ref[...])

    n_sub = x_ref.shape[0] // MLA_SUB
    lat_next = latent(0)
    for sub in range(n_sub):
        rows = slice(sub * MLA_SUB, (sub + 1) * MLA_SUB)
        lat = lat_next
        if sub + 1 < n_sub:
            lat_next = latent(sub + 1)
        cq = _rms(lat[:, :o1], qn_ref[...]).astype(BF16)
        ckv = _rms(lat[:, o1:o2], kvn_ref[...]).astype(BF16)
        pe = lat[:, o2:o2 + ATT_PAD]
        pe_rot = lat[:, o2 + ATT_PAD:o2 + 2 * ATT_PAD]
        ang = pos_ref[rows, :].astype(F32) * inv_ref[...]
        cos = jnp.cos(ang)
        sin = jnp.sin(ang)
        k_pe = pe * cos + pe_rot * sin
        k_ref[rows, :] = (_dot(ckv, wk_ref[...]) + jnp.tile(k_pe, reps)).astype(BF16)
        q = (_dot(cq, wq_ref[...]) * jnp.tile(cos, reps)
             + _dot(cq, wqr_ref[...]) * jnp.tile(sin, reps))
        q_ref[rows, :] = q.astype(BF16)
        v_ref[rows, :] = _dot(ckv, wv_ref[...]).astype(BF16)


def _mla_proj(x2, pos2, gpre, wlat, qn, kvn, wq, wqr, wk, wv, inv):
    t = x2.shape[0]
    tm = MLA_TOKENS
    hp = MLA_HEADS * ATT_PAD
    hv = MLA_HEADS * MLA_V

    def tok(w):
        return pl.BlockSpec((tm, w), lambda i: (i, 0))

    return pl.pallas_call(
        _mla_proj_body,
        name="mla_proj",
        grid=(t // tm,),
        in_specs=[tok(D_MODEL), tok(1), _const_spec((1, D_MODEL)), _const_spec(wlat.shape),
                  _const_spec((1, MLA_Q_LORA)), _const_spec((1, MLA_KV_LORA)),
                  _const_spec((MLA_Q_LORA, hp)), _const_spec((MLA_Q_LORA, hp)),
                  _const_spec((MLA_KV_LORA, hp)), _const_spec((MLA_KV_LORA, hv)),
                  _const_spec((1, ATT_PAD))],
        out_specs=[tok(hp), tok(hp), tok(hv)],
        out_shape=[jax.ShapeDtypeStruct((t, hp), BF16), jax.ShapeDtypeStruct((t, hp), BF16),
                   jax.ShapeDtypeStruct((t, hv), BF16)],
        compiler_params=_params(1),
    )(x2, pos2, gpre, wlat, qn, kvn, wq, wqr, wk, wv, inv)


def _attn_body(q_ref, k_ref, v_ref, o_ref, s_buf):
    s_len = q_ref.shape[0]
    tq = ATT_Q
    c2 = (MLA_NOPE + MLA_ROPE) ** -0.5 * math.log2(math.e)
    ri = lax.broadcasted_iota(jnp.int32, (tq, tq), 0) // CHUNK
    ci = lax.broadcasted_iota(jnp.int32, (tq, tq), 1) // CHUNK
    diag_mask = ci <= ri
    lane = lax.broadcasted_iota(jnp.int32, (tq, 2 * MLA_V), 1)

    def scores(i, e):
        lo = i * tq
        qrows = slice(lo, lo + tq)
        cols = slice(e * ATT_PAD, (e + 1) * ATT_PAD)
        sb = s_buf.at[2 * (i % 2) + e]
        q = q_ref[qrows, cols]
        sd = jnp.where(diag_mask, _dot_nt(q, k_ref[qrows, cols]), -jnp.inf)
        m = jnp.max(sd, axis=-1, keepdims=True)
        if lo:
            sb[:, 0:lo] = _dot_nt(q, k_ref[0:lo, cols])
            m = jnp.maximum(m, jnp.max(sb[:, 0:lo], axis=-1, keepdims=True))
        return sd, m

    def weighted_values(i, e, sd, m):
        lo = i * tq
        qrows = slice(lo, lo + tq)
        sb = s_buf.at[2 * (i % 2) + e]
        pd = jnp.exp2((sd - m) * c2)
        l = jnp.sum(pd, axis=-1, keepdims=True)
        acc = _dot(pd.astype(BF16), v_ref[qrows, :])
        if lo:
            po = jnp.exp2((sb[:, 0:lo] - m) * c2)
            l = l + jnp.sum(po, axis=-1, keepdims=True)
            acc = acc + _dot(po.astype(BF16), v_ref[0:lo, :])
        return acc / l

    units = [(i, e) for i in reversed(range(s_len // tq)) for e in range(2)]
    pending = [scores(*u) for u in units[:ATT_AHEAD]]
    outs = {}
    for n, (i, e) in enumerate(units):
        sd, m = pending.pop(0)
        if n + ATT_AHEAD < len(units):
            pending.append(scores(*units[n + ATT_AHEAD]))
        outs[e] = weighted_values(i, e, sd, m)
        if e == 1:
            qrows = slice(i * tq, (i + 1) * tq)
            o_ref[qrows, :] = jnp.where(lane < MLA_V, outs[0], outs[1]).astype(BF16)


def _attention(q3, k3, v3):
    b, s, _ = q3.shape
    return pl.pallas_call(
        _attn_body,
        name="mla_attn",
        grid=(b, MLA_HEADS // 2),
        in_specs=[pl.BlockSpec((None, s, 2 * ATT_PAD), lambda bi, hp: (bi, 0, hp)),
                  pl.BlockSpec((None, s, 2 * ATT_PAD), lambda bi, hp: (bi, 0, hp)),
                  pl.BlockSpec((None, s, 2 * MLA_V), lambda bi, hp: (bi, 0, hp))],
        out_specs=pl.BlockSpec((None, s, 2 * MLA_V), lambda bi, hp: (bi, 0, hp)),
        out_shape=jax.ShapeDtypeStruct((b, s, MLA_HEADS * MLA_V), BF16),
        scratch_shapes=[pltpu.VMEM((4, ATT_Q, s - ATT_Q), F32)],
        compiler_params=_params(2),
    )(q3, k3, v3)


def _out_proj_body(y_ref, x_ref, w_ref, gpost_ref, o_ref):
    o_ref[...] = x_ref[...] + _rms(_dot(y_ref[...], w_ref[...]), gpost_ref[...])


def _out_proj(y2, x2, w, gpost):
    t, kdim = y2.shape
    tm = OUT_TOKENS
    return pl.pallas_call(
        _out_proj_body,
        name="out_proj",
        grid=(t // tm,),
        in_specs=[pl.BlockSpec((tm, kdim), lambda i: (i, 0)),
                  pl.BlockSpec((tm, D_MODEL), lambda i: (i, 0)),
                  _const_spec((kdim, D_MODEL)), _const_spec((1, D_MODEL))],
        out_specs=pl.BlockSpec((tm, D_MODEL), lambda i: (i, 0)),
        out_shape=jax.ShapeDtypeStruct((t, D_MODEL), F32),
        compiler_params=_params(1),
    )(y2, x2, w, gpost)


def _row(v):
    return v.reshape(1, -1).astype(F32)


def _pad_heads(w, width, offset=0):
    k, hh, d = w.shape
    out = jnp.zeros((k, hh, width), w.dtype)
    out = out.at[:, :, offset:offset + d].set(w)
    return out.reshape(k, hh * width)


def _rot_half_cols(w):
    half = w.shape[-1] // 2
    return jnp.concatenate([-w[..., half:], w[..., :half]], axis=-1)


def _mla_layer(x2, pos2, b, s, gpre, gpost, w_in, q_norm, w_uq, kv_norm, w_ukv, w_o):
    o1 = MLA_Q_LORA
    o2 = o1 + MLA_KV_LORA
    w_pe = w_in[:, o2:]
    slab = jnp.zeros((D_MODEL, ATT_PAD), F32).at[:, MLA_NOPE:MLA_NOPE + MLA_ROPE]
    wlat = jnp.concatenate([w_in[:, :o2], slab.set(w_pe), slab.set(_rot_half_cols(w_pe))],
                           axis=1).astype(BF16)
    wq3 = w_uq.reshape(o1, MLA_HEADS, MLA_NOPE + MLA_ROPE)
    wq = _pad_heads(wq3, ATT_PAD).astype(BF16)
    wqr = _pad_heads(_rot_half_cols(wq3[:, :, MLA_NOPE:]), ATT_PAD, MLA_NOPE).astype(BF16)
    wkv3 = w_ukv.reshape(MLA_KV_LORA, MLA_HEADS, MLA_NOPE + MLA_V)
    wk = _pad_heads(wkv3[:, :, :MLA_NOPE], ATT_PAD).astype(BF16)
    wv = wkv3[:, :, MLA_NOPE:].reshape(MLA_KV_LORA, MLA_HEADS * MLA_V).astype(BF16)
    half = MLA_ROPE // 2
    inv = ROPE_BASE ** (-jnp.arange(half, dtype=F32) / half)
    inv_slab = jnp.zeros((1, ATT_PAD), F32).at[0, MLA_NOPE:MLA_NOPE + MLA_ROPE].set(
        jnp.concatenate([inv, inv]))
    q, k, v = _mla_proj(x2, pos2, _row(gpre), wlat, _row(q_norm), _row(kv_norm), wq, wqr, wk, wv,
                        inv_slab)
    o = _attention(q.reshape(b, s, -1), k.reshape(b, s, -1), v.reshape(b, s, -1))
    return _out_proj(o.reshape(b * s, -1), x2, w_o.astype(BF16), _row(gpost))


def kernel(x, positions, norm_mix_pre, norm_mix_post, norm_ffn_pre, norm_ffn_post, ffn_w_in, ffn_w_out, conv_w_pw1, conv_b_pw1, conv_w_dw, conv_b_dw, conv_ln_g, conv_ln_b, conv_w_pw2, conv_b_pw2, ssm_w_in, ssm_conv_w, ssm_conv_b, ssm_dt_bias, ssm_a_log, ssm_d, ssm_norm_w, ssm_w_out, mla_w_in, mla_q_norm, mla_w_uq, mla_kv_norm, mla_w_ukv, mla_w_o):
    b, s, d = x.shape
    t = b * s
    pos2 = positions.reshape(t, 1)
    ffn_w1 = ffn_w_in.astype(BF16)
    ffn_w2 = ffn_w_out.astype(BF16)
    conv_w1 = conv_w_pw1.astype(BF16)
    conv_w2 = conv_w_pw2.astype(BF16)
    i_conv = i_ssm = i_mla = 0
    for i in range(DEPTH):
        kind = i % N_MIXERS
        gpre, gpost = _row(norm_mix_pre[i]), _row(norm_mix_post[i])
        if kind == 0:
            jx = i_conv
            x = _conv_layer(
                x.reshape(b, s, d), gpre, gpost, conv_w1, _row(conv_b_pw1[jx]),
                conv_w_dw[jx], _row(conv_b_dw[jx]), _row(conv_ln_g[jx]), _row(conv_ln_b[jx]),
                conv_w2, _row(conv_b_pw2[jx]), jx)
            i_conv += 1
        elif kind == 1:
            jx = i_ssm
            w_in = ssm_w_in[jx].astype(BF16)
            o2 = SSM_D_INNER + SSM_CONV_DIM
            rep = SSM_HEAD_DIM
            pad = LANES - SSM_HEADS
            expand = jnp.repeat(jnp.eye(LANES, SSM_HEADS, dtype=BF16), rep, axis=1)
            x = _ssd_layer(
                x.reshape(b, s, d), gpre, gpost, w_in, jnp.pad(w_in[:, o2:], ((0, 0), (0, pad))),
                ssm_conv_w[jx], _row(ssm_conv_b[jx]), _row(jnp.pad(ssm_dt_bias[jx], (0, pad))),
                expand, _row(jnp.pad(ssm_a_log[jx], (0, pad))),
                _row(jnp.repeat(ssm_a_log[jx], rep)), _row(jnp.repeat(ssm_d[jx], rep)),
                _row(ssm_norm_w[jx]), ssm_w_out[jx].astype(BF16))
            i_ssm += 1
        else:
            jx = i_mla
            x = _mla_layer(x.reshape(t, d), pos2, b, s, norm_mix_pre[i], norm_mix_post[i],
                           mla_w_in[jx], mla_q_norm[jx], mla_w_uq[jx], mla_kv_norm[jx],
                           mla_w_ukv[jx], mla_w_o[jx])
            i_mla += 1
        x = _ffn(x.reshape(t, d), _row(norm_ffn_pre[i]), _row(norm_ffn_post[i]), ffn_w1, ffn_w2, i)
    return x.reshape(b, s, d)
```

```python
import math

import jax
import jax.numpy as jnp
from jax import lax
from jax.experimental import pallas as pl
from jax.experimental.pallas import tpu as pltpu

F32 = jnp.float32
BF16 = jnp.bfloat16

D_MODEL = 1024
DEPTH = 4
CHUNK = 64
N_MIXERS = 3
EPS = 1e-6
FFN_DIM = 4 * D_MODEL
CONV_KERNEL = 31
SSM_D_INNER = 2 * D_MODEL
SSM_HEAD_DIM = 64
SSM_HEADS = SSM_D_INNER // SSM_HEAD_DIM
SSM_GROUPS = 8
SSM_STATE = 128
SSM_CONV = 4
SSM_CONV_DIM = SSM_D_INNER + 2 * SSM_GROUPS * SSM_STATE
MLA_HEADS = D_MODEL // 64
MLA_NOPE = 64
MLA_ROPE = 32
MLA_V = 64
MLA_Q_LORA = 3 * D_MODEL // 8
MLA_KV_LORA = D_MODEL // 4
ROPE_BASE = 10000.0
LOG2E = math.log2(math.e)

LANES = 128
SUBLANES = 8
VMEM_LIMIT_BYTES = 56 * 1024 * 1024

FFN_TOKENS = 1024
FFN_SUB = 512
FFN_CHUNK = 512
CONV_TOKENS = 512
CONV_HALO = 32
CONV_ROWS = 32
CONV_LN_ROWS = 128
CONV_LANES = 256
SSD_TOKENS = 512
SSD_CHUNK = 128
SSD_HALO = SUBLANES
SSD_CONV_LANES = 512
SSD_AHEAD = 1
MLA_TOKENS = 512
MLA_SUB = 256
ROPE_PACK = 4
ATT_Q = 256
ATT_PAD = 128
ATT_AHEAD = 2
OUT_TOKENS = 1024


def _const_spec(shape):
    nd = len(shape)
    return pl.BlockSpec(shape, lambda *_: (0,) * nd, pipeline_mode=pl.Buffered(1))


def _layer_spec(shape, layer):
    nd = len(shape)
    return pl.BlockSpec((None,) + tuple(shape), lambda *_: (layer,) + (0,) * nd,
                        pipeline_mode=pl.Buffered(1))


def _params(n_axes, parallel=True):
    sem = ("parallel" if parallel else "arbitrary",) * n_axes
    return pltpu.CompilerParams(dimension_semantics=sem, vmem_limit_bytes=VMEM_LIMIT_BYTES)


def _rms(x, g):
    return x * lax.rsqrt(jnp.mean(x * x, axis=-1, keepdims=True) + EPS) * g


def _silu(x):
    half = 0.5 * x
    return half + half * jnp.tanh(half)


def _dot(a, b):
    return jnp.dot(a, b, preferred_element_type=F32)


def _dot_nt(a, b):
    return lax.dot_general(a, b, (((1,), (1,)), ((), ())), preferred_element_type=F32)


def _dot_tn(a, b):
    return lax.dot_general(a, b, (((0,), (0,)), ((), ())), preferred_element_type=F32)


def _split3(x):
    hi = x.astype(BF16)
    r1 = x - hi.astype(F32)
    mid = r1.astype(BF16)
    lo = (r1 - mid.astype(F32)).astype(BF16)
    return hi, mid, lo


def _dot_exact01(m01, x):
    hi, mid, lo = _split3(x)
    return (_dot(jnp.concatenate([m01, m01], axis=1), jnp.concatenate([hi, mid], axis=0))
            + _dot(m01, lo))


def _dot_exact01_rhs(x, m01):
    hi, mid, lo = _split3(x)
    return (_dot(jnp.concatenate([hi, mid], axis=1), jnp.concatenate([m01, m01], axis=0))
            + _dot(lo, m01))


def _ffn_body(x_ref, gpre_ref, gpost_ref, w1_ref, w2_ref, o_ref):
    for sub in range(FFN_TOKENS // FFN_SUB):
        rows = slice(sub * FFN_SUB, (sub + 1) * FFN_SUB)
        x = x_ref[rows, :]
        h = _rms(x, gpre_ref[...]).astype(BF16)
        acc = None
        for c in range(FFN_DIM // FFN_CHUNK):
            cols = slice(c * FFN_CHUNK, (c + 1) * FFN_CHUNK)
            a = _dot(h, w1_ref[:, cols])
            a = jnp.square(jnp.maximum(a, 0.0)).astype(BF16)
            p = _dot(a, w2_ref[cols, :])
            acc = p if acc is None else acc + p
        o_ref[rows, :] = x + _rms(acc, gpost_ref[...])


def _ffn(x2, gpre, gpost, w1_all, w2_all, layer):
    t = x2.shape[0]
    tok = pl.BlockSpec((FFN_TOKENS, D_MODEL), lambda i: (i, 0))
    return pl.pallas_call(
        _ffn_body,
        name="ffn",
        grid=(t // FFN_TOKENS,),
        in_specs=[tok, _const_spec((1, D_MODEL)), _const_spec((1, D_MODEL)),
                  _layer_spec((D_MODEL, FFN_DIM), layer), _layer_spec((FFN_DIM, D_MODEL), layer)],
        out_specs=tok,
        out_shape=jax.ShapeDtypeStruct((t, D_MODEL), F32),
        compiler_params=_params(1),
    )(x2, gpre, gpost, w1_all, w2_all)


def _conv_body(x_ref, gpre_ref, gpost_ref, w1_ref, b1_ref, wdw_ref, bdw_ref,
               lng_ref, lnb_ref, w2_ref, b2_ref, o_ref, ubuf, shifted, dbuf):
    ts = x_ref.shape[0]
    j = pl.program_id(1)

    @pl.when(j == 0)
    def _():
        ubuf[0:CONV_HALO, :] = jnp.zeros((CONV_HALO, D_MODEL), F32)

    @pl.when(j > 0)
    def _():
        ubuf[0:CONV_HALO, :] = ubuf[ts:ts + CONV_HALO, :]

    x = x_ref[...]
    h = _rms(x, gpre_ref[...]).astype(BF16)
    first = CONV_HALO - (CONV_KERNEL - 1)
    n_sh = CONV_HALO + ts - SUBLANES

    for cblk in range(D_MODEL // CONV_LANES):
        cs = slice(cblk * CONV_LANES, (cblk + 1) * CONV_LANES)
        gs = slice(D_MODEL + cblk * CONV_LANES, D_MODEL + (cblk + 1) * CONV_LANES)
        ua = _dot(h, w1_ref[:, cs]) + b1_ref[:, cs]
        ub = _dot(h, w1_ref[:, gs]) + b1_ref[:, gs]
        ua_half = 0.5 * ua
        ubuf[CONV_HALO:CONV_HALO + ts, cs] = ua_half + ua_half * jnp.tanh(0.5 * ub)
        window = ubuf[:, cs]
        for sft in range(1, SUBLANES):
            rolled = pltpu.roll(window, CONV_HALO + ts - sft, axis=0)
            shifted[sft - 1, 0:n_sh, cs] = rolled[0:n_sh, :]
        for sub in range(CONV_LANES // LANES):
            ls = slice(cblk * CONV_LANES + sub * LANES, cblk * CONV_LANES + (sub + 1) * LANES)
            for rb in range(ts // CONV_ROWS):
                acc = jnp.broadcast_to(bdw_ref[:, ls], (CONV_ROWS, LANES))
                for k in range(CONV_KERNEL):
                    lo = rb * CONV_ROWS + first + k
                    sft = lo % SUBLANES
                    if sft:
                        tap = shifted[sft - 1, lo - sft:lo - sft + CONV_ROWS, ls]
                    else:
                        tap = ubuf[lo:lo + CONV_ROWS, ls]
                    acc = acc + wdw_ref[k:k + 1, ls] * tap
                dbuf[rb * CONV_ROWS:(rb + 1) * CONV_ROWS, ls] = acc

    for rt in range(ts // CONV_LN_ROWS):
        rows = slice(rt * CONV_LN_ROWS, (rt + 1) * CONV_LN_ROWS)
        acc = dbuf[rows, :]
        mu = jnp.mean(acc, axis=-1, keepdims=True)
        cen = acc - mu
        var = jnp.mean(cen * cen, axis=-1, keepdims=True)
        y = cen * lax.rsqrt(var + EPS) * lng_ref[...] + lnb_ref[...]
        v = _silu(y).astype(BF16)
        out = _dot(v, w2_ref[...]) + b2_ref[...]
        o_ref[rows, :] = x[rows, :] + _rms(out, gpost_ref[...])


def _conv_layer(x3, gpre, gpost, w1_all, b1, wdw, bdw, lng, lnb, w2_all, b2, layer):
    b, s, _ = x3.shape
    ts = CONV_TOKENS
    tok = pl.BlockSpec((None, ts, D_MODEL), lambda bi, j: (bi, j, 0))
    vec = _const_spec((1, D_MODEL))
    return pl.pallas_call(
        _conv_body,
        name="conv_mixer",
        grid=(b, s // ts),
        in_specs=[tok, vec, vec, _layer_spec((D_MODEL, 2 * D_MODEL), layer),
                  _const_spec((1, 2 * D_MODEL)), _const_spec((CONV_KERNEL, D_MODEL)), vec,
                  vec, vec, _layer_spec((D_MODEL, D_MODEL), layer), vec],
        out_specs=tok,
        out_shape=jax.ShapeDtypeStruct(x3.shape, F32),
        scratch_shapes=[pltpu.VMEM((CONV_HALO + ts, D_MODEL), F32),
                        pltpu.VMEM((SUBLANES - 1, CONV_HALO + ts, D_MODEL), F32),
                        pltpu.VMEM((ts, D_MODEL), F32)],
        compiler_params=_params(2, parallel=False),
    )(x3, gpre, gpost, w1_all, b1, wdw, bdw, lng, lnb, w2_all, b2)


def _ssd_body(x_ref, gpre_ref, gpost_ref, win_ref, wdt_ref, cw_ref, cb_ref, dtb_ref,
              expand_ref, alogh_ref, alog_ref, dskip_ref, nw_ref, wout_ref, o_ref, hbuf, xbuf,
              dtbuf, zbuf, ybuf, state):
    ts = x_ref.shape[0]
    lc = SSD_CHUNK
    gw = SSM_D_INNER // SSM_GROUPS
    j = pl.program_id(1)

    @pl.when(j == 0)
    def _():
        hbuf[...] = jnp.zeros(hbuf.shape, F32)
        state[...] = jnp.zeros(state.shape, F32)

    x = x_ref[...]
    h = _rms(x, gpre_ref[...]).astype(BF16)
    dt_raw = _dot(h, wdt_ref[...]) + dtb_ref[...]
    dt_heads = jnp.maximum(dt_raw, 0.0) + jnp.log1p(jnp.exp(-jnp.abs(dt_raw)))
    dtbuf[...] = _dot_exact01_rhs(dt_heads, expand_ref[...])

    n_cblk = SSM_CONV_DIM // SSD_CONV_LANES

    def project(cblk):
        ws = slice(SSM_D_INNER + cblk * SSD_CONV_LANES, SSM_D_INNER + (cblk + 1) * SSD_CONV_LANES)
        return _dot(h, win_ref[:, ws])

    raw = project(0)
    for cblk in range(n_cblk):
        cs = slice(cblk * SSD_CONV_LANES, (cblk + 1) * SSD_CONV_LANES)
        if cblk + 1 < n_cblk:
            raw_next = project(cblk + 1)
        else:
            raw_next = None
            zbuf[...] = _dot(h, win_ref[:, 0:SSM_D_INNER])
        window = jnp.concatenate([hbuf[:, cs], raw], axis=0)
        hbuf[:, cs] = raw[ts - SSD_HALO:ts, :]
        raw = raw_next
        conv = jnp.broadcast_to(cb_ref[:, cs], (ts, SSD_CONV_LANES))
        for k in range(SSM_CONV):
            back = SSM_CONV - 1 - k
            tap = pltpu.roll(window, back, axis=0) if back else window
            conv = conv + cw_ref[k:k + 1, cs] * tap[SSD_HALO:SSD_HALO + ts, :]
        xbuf[:, cs] = _silu(conv)

    a_row = -jnp.exp(alog_ref[...]) * LOG2E
    a_heads = -jnp.exp(alogh_ref[...]) * LOG2E
    ri = lax.broadcasted_iota(jnp.int32, (lc, lc), 0)
    ci = lax.broadcasted_iota(jnp.int32, (lc, lc), 1)
    causal = ci <= ri
    tril01 = causal.astype(BF16)
    off_b = SSM_D_INNER
    off_c = SSM_D_INNER + SSM_GROUPS * SSM_STATE
    lane = lax.broadcasted_iota(jnp.int32, (1, gw), 1)
    head01 = [((lane >= r * SSM_HEAD_DIM) & (lane < (r + 1) * SSM_HEAD_DIM)).astype(BF16)
              for r in range(gw // SSM_HEAD_DIM)]

    def chunk_terms(c):
        rows = slice(c * lc, (c + 1) * lc)
        dt = dtbuf[rows, :]
        a_cs = _dot_exact01(tril01, dt * a_row)
        a_last = a_cs[lc - 1:lc, :]
        a_csh = _dot_exact01(tril01, dt_heads[rows, :] * a_heads)
        return dict(rows=rows, dt=dt, exp_a=jnp.exp2(a_cs), to_end=jnp.exp2(a_last - a_cs),
                    decay_chunk=jnp.exp2(a_last), a_csh=a_csh, a_csh_t=a_csh.T)

    def group_inputs(ck, g):
        rows = ck["rows"]
        gl = slice(g * gw, (g + 1) * gw)
        xs = xbuf[rows, gl]
        xdt = xs * ck["dt"][:, gl]
        bm = xbuf[rows, off_b + g * SSM_STATE:off_b + (g + 1) * SSM_STATE].astype(BF16)
        cm = xbuf[rows, off_c + g * SSM_STATE:off_c + (g + 1) * SSM_STATE].astype(BF16)
        return xs, xdt, bm, cm, _dot_nt(cm, bm)

    def group_outputs(ck, g, xs, xdt, bm, cm, cb):
        rows = ck["rows"]
        gl = slice(g * gw, (g + 1) * gw)
        xdb = xdt.astype(BF16)
        ms, xds = [], []
        for r in range(gw // SSM_HEAD_DIM):
            hd = g * (gw // SSM_HEAD_DIM) + r
            col = ck["a_csh"][:, hd:hd + 1]
            row = ck["a_csh_t"][hd:hd + 1, :]
            decay = jnp.where(causal, jnp.exp2(col - row), 0.0)
            ms.append((cb * decay).astype(BF16))
            xds.append(xdb * head01[r])
        y = (_dot(jnp.concatenate(ms[0:2], axis=1), jnp.concatenate(xds[0:2], axis=0))
             + _dot(jnp.concatenate(ms[2:4], axis=1), jnp.concatenate(xds[2:4], axis=0)))
        st = state[g]
        y = y + _dot(cm, st.astype(BF16)) * ck["exp_a"][:, gl]
        xw = (xdt * ck["to_end"][:, gl]).astype(BF16)
        state[g] = st * ck["decay_chunk"][:, gl] + _dot_tn(bm, xw)
        y = y + dskip_ref[:, gl] * xs
        y = y * _silu(zbuf[rows, gl])
        y = y * lax.rsqrt(jnp.mean(y * y, axis=-1, keepdims=True) + EPS) * nw_ref[:, gl]
        ybuf[rows, gl] = y.astype(BF16)

    units = [(c, g) for c in range(ts // lc) for g in range(SSM_GROUPS)]
    chunks = {}

    def issue(n):
        c1, g1 = units[n]
        if c1 not in chunks:
            chunks[c1] = chunk_terms(c1)
        return group_inputs(chunks[c1], g1)

    pending = [issue(n) for n in range(SSD_AHEAD)]
    for n, (c, g) in enumerate(units):
        current = pending.pop(0)
        if n + SSD_AHEAD < len(units):
            pending.append(issue(n + SSD_AHEAD))
        group_outputs(chunks[c], g, *current)
        if g == SSM_GROUPS - 1:
            rows = chunks[c]["rows"]
            out = _dot(ybuf[rows, :], wout_ref[...])
            o_ref[rows, :] = x[rows, :] + _rms(out, gpost_ref[...])


def _ssd_layer(x3, gpre, gpost, win, wdt, cw, cb, dtb, expand, alogh, alog, dskip, nw, wout):
    b, s, _ = x3.shape
    ts = SSD_TOKENS
    tok = pl.BlockSpec((None, ts, D_MODEL), lambda bi, j: (bi, j, 0))
    vec = _const_spec((1, D_MODEL))
    inner = _const_spec((1, SSM_D_INNER))
    return pl.pallas_call(
        _ssd_body,
        name="ssd_mixer",
        grid=(b, s // ts),
        in_specs=[tok, vec, vec, _const_spec(win.shape), _const_spec((D_MODEL, LANES)),
                  _const_spec((SSM_CONV, SSM_CONV_DIM)), _const_spec((1, SSM_CONV_DIM)),
                  _const_spec((1, LANES)), _const_spec((LANES, SSM_D_INNER)), _const_spec((1, LANES)),
                  inner, inner, inner, _const_spec((SSM_D_INNER, D_MODEL))],
        out_specs=tok,
        out_shape=jax.ShapeDtypeStruct(x3.shape, F32),
        scratch_shapes=[pltpu.VMEM((SSD_HALO, SSM_CONV_DIM), F32),
                        pltpu.VMEM((ts, SSM_CONV_DIM), F32),
                        pltpu.VMEM((ts, SSM_D_INNER), F32),
                        pltpu.VMEM((ts, SSM_D_INNER), F32),
                        pltpu.VMEM((ts, SSM_D_INNER), BF16),
                        pltpu.VMEM((SSM_GROUPS, SSM_STATE, SSM_D_INNER // SSM_GROUPS), F32)],
        compiler_params=_params(2, parallel=False),
    )(x3, gpre, gpost, win, wdt, cw, cb, dtb, expand, alogh, alog, dskip, nw, wout)


def _mla_proj_body(x_ref, pos_ref, gpre_ref, wlat_ref, qn_ref, kvn_ref, wq_ref, wqr_ref, wk_ref,
                   wv_ref, inv_ref, q_ref, k_ref, v_ref):
    o1 = MLA_Q_LORA
    o2 = o1 + MLA_KV_LORA
    reps = (1, MLA_HEADS)

    def latent(sub):
        rows = slice(sub * MLA_SUB, (sub + 1) * MLA_SUB)
        h = _rms(x_ref[rows, :], gpre_ref[...]).astype(BF16)
        return _dot(h, wlat_ref[...])

    packed = MLA_SUB // ROPE_PACK
    rope_lane = lax.broadcasted_iota(jnp.int32, (packed, ATT_PAD), 1)
    rope_lane = (rope_lane >= MLA_NOPE) & (rope_lane < MLA_NOPE + MLA_ROPE)

    def rope_terms(sub):
        ang = pos_ref[sub * packed:(sub + 1) * packed, :].astype(F32) * inv_ref[...]
        cos_p, sin_p = jnp.cos(ang), jnp.sin(ang)
        cos, sin = [], []
        for q in range(ROPE_PACK):
            shift = (MLA_NOPE - MLA_ROPE * q) % ATT_PAD
            cos_q = pltpu.roll(cos_p, shift, axis=1) if shift else cos_p
            sin_q = pltpu.roll(sin_p, shift, axis=1) if shift else sin_p
            cos.append(jnp.where(rope_lane, cos_q, 1.0))
            sin.append(jnp.where(rope_lane, sin_q, 0.0))
        return jnp.concatenate(cos, axis=0), jnp.concatenate(sin, axis=0)

    n_sub = x_ref.shape[0] // MLA_SUB
    lat_next = latent(0)
    rope_next = rope_terms(0)
    for sub in range(n_sub):
        rows = slice(sub * MLA_SUB, (sub + 1) * MLA_SUB)
        lat, (cos, sin) = lat_next, rope_next
        if sub + 1 < n_sub:
            lat_next = latent(sub + 1)
            rope_next = rope_terms(sub + 1)
        cq = _rms(lat[:, :o1], qn_ref[...]).astype(BF16)
        ckv = _rms(lat[:, o1:o2], kvn_ref[...]).astype(BF16)
        pe = lat[:, o2:o2 + ATT_PAD]
        pe_rot = lat[:, o2 + ATT_PAD:o2 + 2 * ATT_PAD]
        k_pe = pe * cos + pe_rot * sin
        k_ref[rows, :] = (_dot(ckv, wk_ref[...]) + jnp.tile(k_pe, reps)).astype(BF16)
        q = (_dot(cq, wq_ref[...]) * jnp.tile(cos, reps)
             + _dot(cq, wqr_ref[...]) * jnp.tile(sin, reps))
        q_ref[rows, :] = q.astype(BF16)
        v_ref[rows, :] = _dot(ckv, wv_ref[...]).astype(BF16)


def _mla_proj(x2, pos2, gpre, wlat, qn, kvn, wq, wqr, wk, wv, inv):
    t = x2.shape[0]
    tm = MLA_TOKENS
    hp = MLA_HEADS * ATT_PAD
    hv = MLA_HEADS * MLA_V

    def tok(w):
        return pl.BlockSpec((tm, w), lambda i: (i, 0))

    return pl.pallas_call(
        _mla_proj_body,
        name="mla_proj",
        grid=(t // tm,),
        in_specs=[tok(D_MODEL), pl.BlockSpec((tm // ROPE_PACK, ATT_PAD), lambda i: (i, 0)),
                  _const_spec((1, D_MODEL)), _const_spec(wlat.shape),
                  _const_spec((1, MLA_Q_LORA)), _const_spec((1, MLA_KV_LORA)),
                  _const_spec((MLA_Q_LORA, hp)), _const_spec((MLA_Q_LORA, hp)),
                  _const_spec((MLA_KV_LORA, hp)), _const_spec((MLA_KV_LORA, hv)),
                  _const_spec((1, ATT_PAD))],
        out_specs=[tok(hp), tok(hp), tok(hv)],
        out_shape=[jax.ShapeDtypeStruct((t, hp), BF16), jax.ShapeDtypeStruct((t, hp), BF16),
                   jax.ShapeDtypeStruct((t, hv), BF16)],
        compiler_params=_params(1),
    )(x2, pos2, gpre, wlat, qn, kvn, wq, wqr, wk, wv, inv)


def _attn_body(q_ref, k_ref, v_ref, o_ref, s_buf):
    s_len = q_ref.shape[0]
    tq = ATT_Q
    c2 = (MLA_NOPE + MLA_ROPE) ** -0.5 * math.log2(math.e)
    ri = lax.broadcasted_iota(jnp.int32, (tq, tq), 0) // CHUNK
    ci = lax.broadcasted_iota(jnp.int32, (tq, tq), 1) // CHUNK
    diag_mask = ci <= ri
    lane = lax.broadcasted_iota(jnp.int32, (tq, 2 * MLA_V), 1)

    def scores(i, e):
        lo = i * tq
        qrows = slice(lo, lo + tq)
        cols = slice(e * ATT_PAD, (e + 1) * ATT_PAD)
        sb = s_buf.at[2 * (i % 2) + e]
        q = q_ref[qrows, cols]
        sb[:, qrows] = jnp.where(diag_mask, _dot_nt(q, k_ref[qrows, cols]), -jnp.inf)
        if lo:
            sb[:, 0:lo] = _dot_nt(q, k_ref[0:lo, cols])
        return jnp.max(sb[:, 0:lo + tq], axis=-1, keepdims=True)

    def weighted_values(i, e, m):
        hi = (i + 1) * tq
        sb = s_buf.at[2 * (i % 2) + e]
        p = jnp.exp2((sb[:, 0:hi] - m) * c2)
        l = jnp.sum(p, axis=-1, keepdims=True)
        return _dot(p.astype(BF16), v_ref[0:hi, :]) / l

    units = [(i, e) for i in reversed(range(s_len // tq)) for e in range(2)]
    pending = [scores(*u) for u in units[:ATT_AHEAD]]
    outs = {}
    for n, (i, e) in enumerate(units):
        m = pending.pop(0)
        if n + ATT_AHEAD < len(units):
            pending.append(scores(*units[n + ATT_AHEAD]))
        outs[e] = weighted_values(i, e, m)
        if e == 1:
            qrows = slice(i * tq, (i + 1) * tq)
            o_ref[qrows, :] = jnp.where(lane < MLA_V, outs[0], outs[1]).astype(BF16)


def _attention(q3, k3, v3):
    b, s, _ = q3.shape
    return pl.pallas_call(
        _attn_body,
        name="mla_attn",
        grid=(b, MLA_HEADS // 2),
        in_specs=[pl.BlockSpec((None, s, 2 * ATT_PAD), lambda bi, hp: (bi, 0, hp)),
                  pl.BlockSpec((None, s, 2 * ATT_PAD), lambda bi, hp: (bi, 0, hp)),
                  pl.BlockSpec((None, s, 2 * MLA_V), lambda bi, hp: (bi, 0, hp))],
        out_specs=pl.BlockSpec((None, s, 2 * MLA_V), lambda bi, hp: (bi, 0, hp)),
        out_shape=jax.ShapeDtypeStruct((b, s, MLA_HEADS * MLA_V), BF16),
        scratch_shapes=[pltpu.VMEM((4, ATT_Q, s), F32)],
        compiler_params=_params(2),
    )(q3, k3, v3)


def _out_proj_body(y_ref, x_ref, w_ref, gpost_ref, o_ref):
    o_ref[...] = x_ref[...] + _rms(_dot(y_ref[...], w_ref[...]), gpost_ref[...])


def _out_proj(y2, x2, w, gpost):
    t, kdim = y2.shape
    tm = OUT_TOKENS
    return pl.pallas_call(
        _out_proj_body,
        name="out_proj",
        grid=(t // tm,),
        in_specs=[pl.BlockSpec((tm, kdim), lambda i: (i, 0)),
                  pl.BlockSpec((tm, D_MODEL), lambda i: (i, 0)),
                  _const_spec((kdim, D_MODEL)), _const_spec((1, D_MODEL))],
        out_specs=pl.BlockSpec((tm, D_MODEL), lambda i: (i, 0)),
        out_shape=jax.ShapeDtypeStruct((t, D_MODEL), F32),
        compiler_params=_params(1),
    )(y2, x2, w, gpost)


def _row(v):
    return v.reshape(1, -1).astype(F32)


def _pad_heads(w, width, offset=0):
    k, hh, d = w.shape
    out = jnp.zeros((k, hh, width), w.dtype)
    out = out.at[:, :, offset:offset + d].set(w)
    return out.reshape(k, hh * width)


def _rot_half_cols(w):
    half = w.shape[-1] // 2
    return jnp.concatenate([-w[..., half:], w[..., :half]], axis=-1)


def _mla_layer(x2, pos2, b, s, gpre, gpost, w_in, q_norm, w_uq, kv_norm, w_ukv, w_o):
    o1 = MLA_Q_LORA
    o2 = o1 + MLA_KV_LORA
    w_pe = w_in[:, o2:]
    slab = jnp.zeros((D_MODEL, ATT_PAD), F32).at[:, MLA_NOPE:MLA_NOPE + MLA_ROPE]
    wlat = jnp.concatenate([w_in[:, :o2], slab.set(w_pe), slab.set(_rot_half_cols(w_pe))],
                           axis=1).astype(BF16)
    wq3 = w_uq.reshape(o1, MLA_HEADS, MLA_NOPE + MLA_ROPE)
    wq = _pad_heads(wq3, ATT_PAD).astype(BF16)
    wqr = _pad_heads(_rot_half_cols(wq3[:, :, MLA_NOPE:]), ATT_PAD, MLA_NOPE).astype(BF16)
    wkv3 = w_ukv.reshape(MLA_KV_LORA, MLA_HEADS, MLA_NOPE + MLA_V)
    wk = _pad_heads(wkv3[:, :, :MLA_NOPE], ATT_PAD).astype(BF16)
    wv = wkv3[:, :, MLA_NOPE:].reshape(MLA_KV_LORA, MLA_HEADS * MLA_V).astype(BF16)
    half = MLA_ROPE // 2
    inv = ROPE_BASE ** (-jnp.arange(half, dtype=F32) / half)
    inv_packed = jnp.tile(jnp.concatenate([inv, inv]), ROPE_PACK).reshape(1, ATT_PAD)
    q, k, v = _mla_proj(x2, pos2, _row(gpre), wlat, _row(q_norm), _row(kv_norm), wq, wqr, wk, wv,
                        inv_packed)
    o = _attention(q.reshape(b, s, -1), k.reshape(b, s, -1), v.reshape(b, s, -1))
    return _out_proj(o.reshape(b * s, -1), x2, w_o.astype(BF16), _row(gpost))


def kernel(x, positions, norm_mix_pre, norm_mix_post, norm_ffn_pre, norm_ffn_post, ffn_w_in, ffn_w_out, conv_w_pw1, conv_b_pw1, conv_w_dw, conv_b_dw, conv_ln_g, conv_ln_b, conv_w_pw2, conv_b_pw2, ssm_w_in, ssm_conv_w, ssm_conv_b, ssm_dt_bias, ssm_a_log, ssm_d, ssm_norm_w, ssm_w_out, mla_w_in, mla_q_norm, mla_w_uq, mla_kv_norm, mla_w_ukv, mla_w_o):
    b, s, d = x.shape
    t = b * s
    quarter = MLA_SUB // ROPE_PACK
    pos2 = jnp.repeat(positions.reshape(t // MLA_SUB, ROPE_PACK, quarter).transpose(0, 2, 1),
                      MLA_ROPE, axis=2).reshape(t // ROPE_PACK, ATT_PAD)
    ffn_w1 = ffn_w_in.astype(BF16)
    ffn_w2 = ffn_w_out.astype(BF16)
    conv_w1 = conv_w_pw1.astype(BF16)
    conv_w2 = conv_w_pw2.astype(BF16)
    i_conv = i_ssm = i_mla = 0
    for i in range(DEPTH):
        kind = i % N_MIXERS
        gpre, gpost = _row(norm_mix_pre[i]), _row(norm_mix_post[i])
        if kind == 0:
            jx = i_conv
            x = _conv_layer(
                x.reshape(b, s, d), gpre, gpost, conv_w1, _row(conv_b_pw1[jx]),
                conv_w_dw[jx], _row(conv_b_dw[jx]), _row(conv_ln_g[jx]), _row(conv_ln_b[jx]),
                conv_w2, _row(conv_b_pw2[jx]), jx)
            i_conv += 1
        elif kind == 1:
            jx = i_ssm
            w_in = ssm_w_in[jx].astype(BF16)
            o2 = SSM_D_INNER + SSM_CONV_DIM
            rep = SSM_HEAD_DIM
            pad = LANES - SSM_HEADS
            expand = jnp.repeat(jnp.eye(LANES, SSM_HEADS, dtype=BF16), rep, axis=1)
            x = _ssd_layer(
                x.reshape(b, s, d), gpre, gpost, w_in, jnp.pad(w_in[:, o2:], ((0, 0), (0, pad))),
                ssm_conv_w[jx], _row(ssm_conv_b[jx]), _row(jnp.pad(ssm_dt_bias[jx], (0, pad))),
                expand, _row(jnp.pad(ssm_a_log[jx], (0, pad))),
                _row(jnp.repeat(ssm_a_log[jx], rep)), _row(jnp.repeat(ssm_d[jx], rep)),
                _row(ssm_norm_w[jx]), ssm_w_out[jx].astype(BF16))
            i_ssm += 1
        else:
            jx = i_mla
            x = _mla_layer(x.reshape(t, d), pos2, b, s, norm_mix_pre[i], norm_mix_post[i],
                           mla_w_in[jx], mla_q_norm[jx], mla_w_uq[jx], mla_kv_norm[jx],
                           mla_w_ukv[jx], mla_w_o[jx])
            i_mla += 1
        x = _ffn(x.reshape(t, d), _row(norm_ffn_pre[i]), _row(norm_ffn_post[i]), ffn_w1, ffn_w2, i)
    return x.reshape(b, s, d)
```

```python
import math

import jax
import jax.numpy as jnp
from jax import lax
from jax.experimental import pallas as pl
from jax.experimental.pallas import tpu as pltpu

F32 = jnp.float32
BF16 = jnp.bfloat16

D_MODEL = 1024
DEPTH = 4
CHUNK = 64
N_MIXERS = 3
EPS = 1e-6
FFN_DIM = 4 * D_MODEL
CONV_KERNEL = 31
SSM_D_INNER = 2 * D_MODEL
SSM_HEAD_DIM = 64
SSM_HEADS = SSM_D_INNER // SSM_HEAD_DIM
SSM_GROUPS = 8
SSM_STATE = 128
SSM_CONV = 4
SSM_CONV_DIM = SSM_D_INNER + 2 * SSM_GROUPS * SSM_STATE
MLA_HEADS = D_MODEL // 64
MLA_NOPE = 64
MLA_ROPE = 32
MLA_V = 64
MLA_Q_LORA = 3 * D_MODEL // 8
MLA_KV_LORA = D_MODEL // 4
ROPE_BASE = 10000.0
LOG2E = math.log2(math.e)

LANES = 128
SUBLANES = 8
VMEM_LIMIT_BYTES = 56 * 1024 * 1024

FFN_TOKENS = 1024
FFN_SUB = 512
FFN_CHUNK = 512
CONV_TOKENS = 1024
CONV_HALO = 32
CONV_ROWS = 32
CONV_LN_ROWS = 128
CONV_LANES = 256
SSD_TOKENS = 512
SSD_CHUNK = 128
SSD_HALO = SUBLANES
SSD_CONV_LANES = 512
SSD_AHEAD = 1
MLA_TOKENS = 512
MLA_SUB = 256
ROPE_PACK = 4
ATT_Q = 256
ATT_PAD = 128
ATT_AHEAD = 2
OUT_TOKENS = 1024


def _const_spec(shape):
    nd = len(shape)
    return pl.BlockSpec(shape, lambda *_: (0,) * nd, pipeline_mode=pl.Buffered(1))


def _layer_spec(shape, layer):
    nd = len(shape)
    return pl.BlockSpec((None,) + tuple(shape), lambda *_: (layer,) + (0,) * nd,
                        pipeline_mode=pl.Buffered(1))


def _params(n_axes, parallel=True):
    sem = ("parallel" if parallel else "arbitrary",) * n_axes
    return pltpu.CompilerParams(dimension_semantics=sem, vmem_limit_bytes=VMEM_LIMIT_BYTES)


def _rms(x, g):
    return x * lax.rsqrt(jnp.mean(x * x, axis=-1, keepdims=True) + EPS) * g


def _silu(x):
    half = 0.5 * x
    return half + half * jnp.tanh(half)


def _dot(a, b):
    return jnp.dot(a, b, preferred_element_type=F32)


def _dot_nt(a, b):
    return lax.dot_general(a, b, (((1,), (1,)), ((), ())), preferred_element_type=F32)


def _dot_tn(a, b):
    return lax.dot_general(a, b, (((0,), (0,)), ((), ())), preferred_element_type=F32)


def _split3(x):
    hi = x.astype(BF16)
    r1 = x - hi.astype(F32)
    mid = r1.astype(BF16)
    lo = (r1 - mid.astype(F32)).astype(BF16)
    return hi, mid, lo


def _dot_exact01(m01, x):
    hi, mid, lo = _split3(x)
    return (_dot(jnp.concatenate([m01, m01], axis=1), jnp.concatenate([hi, mid], axis=0))
            + _dot(m01, lo))


def _dot_exact01_rhs(x, m01):
    hi, mid, lo = _split3(x)
    return (_dot(jnp.concatenate([hi, mid], axis=1), jnp.concatenate([m01, m01], axis=0))
            + _dot(lo, m01))


def _ffn_body(x_ref, gpre_ref, gpost_ref, w1_ref, w2_ref, o_ref):
    for sub in range(FFN_TOKENS // FFN_SUB):
        rows = slice(sub * FFN_SUB, (sub + 1) * FFN_SUB)
        x = x_ref[rows, :]
        h = _rms(x, gpre_ref[...]).astype(BF16)
        acc = None
        for c in range(FFN_DIM // FFN_CHUNK):
            cols = slice(c * FFN_CHUNK, (c + 1) * FFN_CHUNK)
            a = _dot(h, w1_ref[:, cols])
            a = jnp.square(jnp.maximum(a, 0.0)).astype(BF16)
            p = _dot(a, w2_ref[cols, :])
            acc = p if acc is None else acc + p
        o_ref[rows, :] = x + _rms(acc, gpost_ref[...])


def _ffn(x2, gpre, gpost, w1_all, w2_all, layer):
    t = x2.shape[0]
    tok = pl.BlockSpec((FFN_TOKENS, D_MODEL), lambda i: (i, 0))
    return pl.pallas_call(
        _ffn_body,
        name="ffn",
        grid=(t // FFN_TOKENS,),
        in_specs=[tok, _const_spec((1, D_MODEL)), _const_spec((1, D_MODEL)),
                  _layer_spec((D_MODEL, FFN_DIM), layer), _layer_spec((FFN_DIM, D_MODEL), layer)],
        out_specs=tok,
        out_shape=jax.ShapeDtypeStruct((t, D_MODEL), F32),
        compiler_params=_params(1),
    )(x2, gpre, gpost, w1_all, w2_all)


def _conv_body(x_ref, gpre_ref, gpost_ref, w1_ref, b1_ref, wdw_ref, bdw_ref,
               lng_ref, lnb_ref, w2_ref, b2_ref, o_ref, ubuf, shifted_ref, dbuf):
    ts = x_ref.shape[0]
    j = pl.program_id(1)

    @pl.when(j == 0)
    def _():
        ubuf[0:CONV_HALO, :] = jnp.zeros((CONV_HALO, D_MODEL), F32)

    @pl.when(j > 0)
    def _():
        ubuf[0:CONV_HALO, :] = ubuf[ts:ts + CONV_HALO, :]

    x = x_ref[...]
    h = _rms(x, gpre_ref[...]).astype(BF16)
    first = CONV_HALO - (CONV_KERNEL - 1)
    n_sh = CONV_HALO + ts - SUBLANES

    for cblk in range(D_MODEL // CONV_LANES):
        shifted = shifted_ref.at[cblk % 2]
        cs = slice(cblk * CONV_LANES, (cblk + 1) * CONV_LANES)
        gs = slice(D_MODEL + cblk * CONV_LANES, D_MODEL + (cblk + 1) * CONV_LANES)
        ua = _dot(h, w1_ref[:, cs]) + b1_ref[:, cs]
        ub = _dot(h, w1_ref[:, gs]) + b1_ref[:, gs]
        ua_half = 0.5 * ua
        ubuf[CONV_HALO:CONV_HALO + ts, cs] = ua_half + ua_half * jnp.tanh(0.5 * ub)
        window = ubuf[:, cs]
        for sft in range(1, SUBLANES):
            rolled = pltpu.roll(window, CONV_HALO + ts - sft, axis=0)
            shifted[sft - 1, 0:n_sh, :] = rolled[0:n_sh, :]
        for sub in range(CONV_LANES // LANES):
            ls = slice(cblk * CONV_LANES + sub * LANES, cblk * CONV_LANES + (sub + 1) * LANES)
            bl = slice(sub * LANES, (sub + 1) * LANES)
            for rb in range(ts // CONV_ROWS):
                acc = jnp.broadcast_to(bdw_ref[:, ls], (CONV_ROWS, LANES))
                for k in range(CONV_KERNEL):
                    lo = rb * CONV_ROWS + first + k
                    sft = lo % SUBLANES
                    if sft:
                        tap = shifted[sft - 1, lo - sft:lo - sft + CONV_ROWS, bl]
                    else:
                        tap = ubuf[lo:lo + CONV_ROWS, ls]
                    acc = acc + wdw_ref[k:k + 1, ls] * tap
                dbuf[rb * CONV_ROWS:(rb + 1) * CONV_ROWS, ls] = acc

    for rt in range(ts // CONV_LN_ROWS):
        rows = slice(rt * CONV_LN_ROWS, (rt + 1) * CONV_LN_ROWS)
        acc = dbuf[rows, :]
        mu = jnp.mean(acc, axis=-1, keepdims=True)
        cen = acc - mu
        var = jnp.mean(cen * cen, axis=-1, keepdims=True)
        y = cen * lax.rsqrt(var + EPS) * lng_ref[...] + lnb_ref[...]
        v = _silu(y).astype(BF16)
        out = _dot(v, w2_ref[...]) + b2_ref[...]
        o_ref[rows, :] = x[rows, :] + _rms(out, gpost_ref[...])


def _conv_layer(x3, gpre, gpost, w1_all, b1, wdw, bdw, lng, lnb, w2_all, b2, layer):
    b, s, _ = x3.shape
    ts = CONV_TOKENS
    tok = pl.BlockSpec((None, ts, D_MODEL), lambda bi, j: (bi, j, 0))
    vec = _const_spec((1, D_MODEL))
    return pl.pallas_call(
        _conv_body,
        name="conv_mixer",
        grid=(b, s // ts),
        in_specs=[tok, vec, vec, _layer_spec((D_MODEL, 2 * D_MODEL), layer),
                  _const_spec((1, 2 * D_MODEL)), _const_spec((CONV_KERNEL, D_MODEL)), vec,
                  vec, vec, _layer_spec((D_MODEL, D_MODEL), layer), vec],
        out_specs=tok,
        out_shape=jax.ShapeDtypeStruct(x3.shape, F32),
        scratch_shapes=[pltpu.VMEM((CONV_HALO + ts, D_MODEL), F32),
                        pltpu.VMEM((2, SUBLANES - 1, CONV_HALO + ts, CONV_LANES), F32),
                        pltpu.VMEM((ts, D_MODEL), F32)],
        compiler_params=_params(2, parallel=False),
    )(x3, gpre, gpost, w1_all, b1, wdw, bdw, lng, lnb, w2_all, b2)


def _ssd_body(x_ref, gpre_ref, gpost_ref, win_ref, wdt_ref, cw_ref, cb_ref, dtb_ref,
              expand_ref, alogh_ref, alog_ref, dskip_ref, nw_ref, wout_ref, o_ref, hbuf, xbuf,
              dtbuf, zbuf, ybuf, state):
    ts = x_ref.shape[0]
    lc = SSD_CHUNK
    gw = SSM_D_INNER // SSM_GROUPS
    j = pl.program_id(1)

    @pl.when(j == 0)
    def _():
        hbuf[...] = jnp.zeros(hbuf.shape, F32)
        state[...] = jnp.zeros(state.shape, F32)

    x = x_ref[...]
    h = _rms(x, gpre_ref[...]).astype(BF16)
    dt_raw = _dot(h, wdt_ref[...]) + dtb_ref[...]
    dt_heads = jnp.maximum(dt_raw, 0.0) + jnp.log1p(jnp.exp(-jnp.abs(dt_raw)))
    dtbuf[...] = _dot_exact01_rhs(dt_heads, expand_ref[...])

    n_cblk = SSM_CONV_DIM // SSD_CONV_LANES

    def project(cblk):
        ws = slice(SSM_D_INNER + cblk * SSD_CONV_LANES, SSM_D_INNER + (cblk + 1) * SSD_CONV_LANES)
        return _dot(h, win_ref[:, ws])

    raw = project(0)
    for cblk in range(n_cblk):
        cs = slice(cblk * SSD_CONV_LANES, (cblk + 1) * SSD_CONV_LANES)
        if cblk + 1 < n_cblk:
            raw_next = project(cblk + 1)
        else:
            raw_next = None
            zbuf[...] = _dot(h, win_ref[:, 0:SSM_D_INNER])
        window = jnp.concatenate([hbuf[:, cs], raw], axis=0)
        hbuf[:, cs] = raw[ts - SSD_HALO:ts, :]
        raw = raw_next
        conv = jnp.broadcast_to(cb_ref[:, cs], (ts, SSD_CONV_LANES))
        for k in range(SSM_CONV):
            back = SSM_CONV - 1 - k
            tap = pltpu.roll(window, back, axis=0) if back else window
            conv = conv + cw_ref[k:k + 1, cs] * tap[SSD_HALO:SSD_HALO + ts, :]
        xbuf[:, cs] = _silu(conv)

    a_row = -jnp.exp(alog_ref[...]) * LOG2E
    a_heads = -jnp.exp(alogh_ref[...]) * LOG2E
    ri = lax.broadcasted_iota(jnp.int32, (lc, lc), 0)
    ci = lax.broadcasted_iota(jnp.int32, (lc, lc), 1)
    causal = ci <= ri
    tril01 = causal.astype(BF16)
    off_b = SSM_D_INNER
    off_c = SSM_D_INNER + SSM_GROUPS * SSM_STATE
    lane = lax.broadcasted_iota(jnp.int32, (1, gw), 1)
    head01 = [((lane >= r * SSM_HEAD_DIM) & (lane < (r + 1) * SSM_HEAD_DIM)).astype(BF16)
              for r in range(gw // SSM_HEAD_DIM)]

    def chunk_terms(c):
        rows = slice(c * lc, (c + 1) * lc)
        dt = dtbuf[rows, :]
        a_cs = _dot_exact01(tril01, dt * a_row)
        a_last = a_cs[lc - 1:lc, :]
        a_csh = _dot_exact01(tril01, dt_heads[rows, :] * a_heads)
        return dict(rows=rows, dt=dt, exp_a=jnp.exp2(a_cs), to_end=jnp.exp2(a_last - a_cs),
                    decay_chunk=jnp.exp2(a_last), a_csh=a_csh, a_csh_t=a_csh.T)

    def group_inputs(ck, g):
        rows = ck["rows"]
        gl = slice(g * gw, (g + 1) * gw)
        xs = xbuf[rows, gl]
        xdt = xs * ck["dt"][:, gl]
        bm = xbuf[rows, off_b + g * SSM_STATE:off_b + (g + 1) * SSM_STATE].astype(BF16)
        cm = xbuf[rows, off_c + g * SSM_STATE:off_c + (g + 1) * SSM_STATE].astype(BF16)
        return xs, xdt, bm, cm, _dot_nt(cm, bm)

    def group_outputs(ck, g, xs, xdt, bm, cm, cb):
        rows = ck["rows"]
        gl = slice(g * gw, (g + 1) * gw)
        xdb = xdt.astype(BF16)
        ms, xds = [], []
        for r in range(gw // SSM_HEAD_DIM):
            hd = g * (gw // SSM_HEAD_DIM) + r
            col = ck["a_csh"][:, hd:hd + 1]
            row = ck["a_csh_t"][hd:hd + 1, :]
            decay = jnp.where(causal, jnp.exp2(col - row), 0.0)
            ms.append((cb * decay).astype(BF16))
            xds.append(xdb * head01[r])
        y = (_dot(jnp.concatenate(ms[0:2], axis=1), jnp.concatenate(xds[0:2], axis=0))
             + _dot(jnp.concatenate(ms[2:4], axis=1), jnp.concatenate(xds[2:4], axis=0)))
        st = state[g]
        y = y + _dot(cm, st.astype(BF16)) * ck["exp_a"][:, gl]
        xw = (xdt * ck["to_end"][:, gl]).astype(BF16)
        state[g] = st * ck["decay_chunk"][:, gl] + _dot_tn(bm, xw)
        y = y + dskip_ref[:, gl] * xs
        y = y * _silu(zbuf[rows, gl])
        y = y * lax.rsqrt(jnp.mean(y * y, axis=-1, keepdims=True) + EPS) * nw_ref[:, gl]
        ybuf[rows, gl] = y.astype(BF16)

    units = [(c, g) for c in range(ts // lc) for g in range(SSM_GROUPS)]
    chunks = {}

    def issue(n):
        c1, g1 = units[n]
        if c1 not in chunks:
            chunks[c1] = chunk_terms(c1)
        return group_inputs(chunks[c1], g1)

    pending = [issue(n) for n in range(SSD_AHEAD)]
    for n, (c, g) in enumerate(units):
        current = pending.pop(0)
        if n + SSD_AHEAD < len(units):
            pending.append(issue(n + SSD_AHEAD))
        group_outputs(chunks[c], g, *current)
        if g == SSM_GROUPS - 1:
            rows = chunks[c]["rows"]
            out = _dot(ybuf[rows, :], wout_ref[...])
            o_ref[rows, :] = x[rows, :] + _rms(out, gpost_ref[...])


def _ssd_layer(x3, gpre, gpost, win, wdt, cw, cb, dtb, expand, alogh, alog, dskip, nw, wout):
    b, s, _ = x3.shape
    ts = SSD_TOKENS
    tok = pl.BlockSpec((None, ts, D_MODEL), lambda bi, j: (bi, j, 0))
    vec = _const_spec((1, D_MODEL))
    inner = _const_spec((1, SSM_D_INNER))
    return pl.pallas_call(
        _ssd_body,
        name="ssd_mixer",
        grid=(b, s // ts),
        in_specs=[tok, vec, vec, _const_spec(win.shape), _const_spec((D_MODEL, LANES)),
                  _const_spec((SSM_CONV, SSM_CONV_DIM)), _const_spec((1, SSM_CONV_DIM)),
                  _const_spec((1, LANES)), _const_spec((LANES, SSM_D_INNER)), _const_spec((1, LANES)),
                  inner, inner, inner, _const_spec((SSM_D_INNER, D_MODEL))],
        out_specs=tok,
        out_shape=jax.ShapeDtypeStruct(x3.shape, F32),
        scratch_shapes=[pltpu.VMEM((SSD_HALO, SSM_CONV_DIM), F32),
                        pltpu.VMEM((ts, SSM_CONV_DIM), F32),
                        pltpu.VMEM((ts, SSM_D_INNER), F32),
                        pltpu.VMEM((ts, SSM_D_INNER), F32),
                        pltpu.VMEM((ts, SSM_D_INNER), BF16),
                        pltpu.VMEM((SSM_GROUPS, SSM_STATE, SSM_D_INNER // SSM_GROUPS), F32)],
        compiler_params=_params(2, parallel=False),
    )(x3, gpre, gpost, win, wdt, cw, cb, dtb, expand, alogh, alog, dskip, nw, wout)


def _mla_proj_body(x_ref, pos_ref, gpre_ref, wlat_ref, qn_ref, kvn_ref, wq_ref, wqr_ref, wk_ref,
                   wv_ref, inv_ref, q_ref, k_ref, v_ref):
    o1 = MLA_Q_LORA
    o2 = o1 + MLA_KV_LORA
    reps = (1, MLA_HEADS)

    def latent(sub):
        rows = slice(sub * MLA_SUB, (sub + 1) * MLA_SUB)
        h = _rms(x_ref[rows, :], gpre_ref[...]).astype(BF16)
        return _dot(h, wlat_ref[...])

    packed = MLA_SUB // ROPE_PACK
    rope_lane = lax.broadcasted_iota(jnp.int32, (packed, ATT_PAD), 1)
    rope_lane = (rope_lane >= MLA_NOPE) & (rope_lane < MLA_NOPE + MLA_ROPE)

    def rope_terms(sub):
        ang = pos_ref[sub * packed:(sub + 1) * packed, :].astype(F32) * inv_ref[...]
        cos_p, sin_p = jnp.cos(ang), jnp.sin(ang)
        cos, sin = [], []
        for q in range(ROPE_PACK):
            shift = (MLA_NOPE - MLA_ROPE * q) % ATT_PAD
            cos_q = pltpu.roll(cos_p, shift, axis=1) if shift else cos_p
            sin_q = pltpu.roll(sin_p, shift, axis=1) if shift else sin_p
            cos.append(jnp.where(rope_lane, cos_q, 1.0))
            sin.append(jnp.where(rope_lane, sin_q, 0.0))
        return jnp.concatenate(cos, axis=0), jnp.concatenate(sin, axis=0)

    n_sub = x_ref.shape[0] // MLA_SUB
    lat_next = latent(0)
    rope_next = rope_terms(0)
    for sub in range(n_sub):
        rows = slice(sub * MLA_SUB, (sub + 1) * MLA_SUB)
        lat, (cos, sin) = lat_next, rope_next
        if sub + 1 < n_sub:
            lat_next = latent(sub + 1)
            rope_next = rope_terms(sub + 1)
        cq = _rms(lat[:, :o1], qn_ref[...]).astype(BF16)
        ckv = _rms(lat[:, o1:o2], kvn_ref[...]).astype(BF16)
        pe = lat[:, o2:o2 + ATT_PAD]
        pe_rot = lat[:, o2 + ATT_PAD:o2 + 2 * ATT_PAD]
        k_pe = pe * cos + pe_rot * sin
        k_ref[rows, :] = (_dot(ckv, wk_ref[...]) + jnp.tile(k_pe, reps)).astype(BF16)
        q = (_dot(cq, wq_ref[...]) * jnp.tile(cos, reps)
             + _dot(cq, wqr_ref[...]) * jnp.tile(sin, reps))
        q_ref[rows, :] = q.astype(BF16)
        v_ref[rows, :] = _dot(ckv, wv_ref[...]).astype(BF16)


def _mla_proj(x2, pos2, gpre, wlat, qn, kvn, wq, wqr, wk, wv, inv):
    t = x2.shape[0]
    tm = MLA_TOKENS
    hp = MLA_HEADS * ATT_PAD
    hv = MLA_HEADS * MLA_V

    def tok(w):
        return pl.BlockSpec((tm, w), lambda i: (i, 0))

    return pl.pallas_call(
        _mla_proj_body,
        name="mla_proj",
        grid=(t // tm,),
        in_specs=[tok(D_MODEL), pl.BlockSpec((tm // ROPE_PACK, ATT_PAD), lambda i: (i, 0)),
                  _const_spec((1, D_MODEL)), _const_spec(wlat.shape),
                  _const_spec((1, MLA_Q_LORA)), _const_spec((1, MLA_KV_LORA)),
                  _const_spec((MLA_Q_LORA, hp)), _const_spec((MLA_Q_LORA, hp)),
                  _const_spec((MLA_KV_LORA, hp)), _const_spec((MLA_KV_LORA, hv)),
                  _const_spec((1, ATT_PAD))],
        out_specs=[tok(hp), tok(hp), tok(hv)],
        out_shape=[jax.ShapeDtypeStruct((t, hp), BF16), jax.ShapeDtypeStruct((t, hp), BF16),
                   jax.ShapeDtypeStruct((t, hv), BF16)],
        compiler_params=_params(1),
    )(x2, pos2, gpre, wlat, qn, kvn, wq, wqr, wk, wv, inv)


def _attn_body(q_ref, k_ref, v_ref, o_ref, s_buf):
    s_len = q_ref.shape[0]
    tq = ATT_Q
    c2 = (MLA_NOPE + MLA_ROPE) ** -0.5 * math.log2(math.e)
    ri = lax.broadcasted_iota(jnp.int32, (tq, tq), 0) // CHUNK
    ci = lax.broadcasted_iota(jnp.int32, (tq, tq), 1) // CHUNK
    diag_mask = ci <= ri
    lane = lax.broadcasted_iota(jnp.int32, (tq, 2 * MLA_V), 1)

    def scores(i, e):
        lo = i * tq
        qrows = slice(lo, lo + tq)
        cols = slice(e * ATT_PAD, (e + 1) * ATT_PAD)
        sb = s_buf.at[2 * (i % 2) + e]
        q = q_ref[qrows, cols]
        sb[:, qrows] = jnp.where(diag_mask, _dot_nt(q, k_ref[qrows, cols]), -jnp.inf)
        if lo:
            sb[:, 0:lo] = _dot_nt(q, k_ref[0:lo, cols])
        return jnp.max(sb[:, 0:lo + tq], axis=-1, keepdims=True)

    def weighted_values(i, e, m):
        hi = (i + 1) * tq
        sb = s_buf.at[2 * (i % 2) + e]
        p = jnp.exp2((sb[:, 0:hi] - m) * c2)
        l = jnp.sum(p, axis=-1, keepdims=True)
        return _dot(p.astype(BF16), v_ref[0:hi, :]) / l

    units = [(i, e) for i in reversed(range(s_len // tq)) for e in range(2)]
    pending = [scores(*u) for u in units[:ATT_AHEAD]]
    outs = {}
    for n, (i, e) in enumerate(units):
        m = pending.pop(0)
        if n + ATT_AHEAD < len(units):
            pending.append(scores(*units[n + ATT_AHEAD]))
        outs[e] = weighted_values(i, e, m)
        if e == 1:
            qrows = slice(i * tq, (i + 1) * tq)
            o_ref[qrows, :] = jnp.where(lane < MLA_V, outs[0], outs[1]).astype(BF16)


def _attention(q3, k3, v3):
    b, s, _ = q3.shape
    return pl.pallas_call(
        _attn_body,
        name="mla_attn",
        grid=(b, MLA_HEADS // 2),
        in_specs=[pl.BlockSpec((None, s, 2 * ATT_PAD), lambda bi, hp: (bi, 0, hp)),
                  pl.BlockSpec((None, s, 2 * ATT_PAD), lambda bi, hp: (bi, 0, hp)),
                  pl.BlockSpec((None, s, 2 * MLA_V), lambda bi, hp: (bi, 0, hp))],
        out_specs=pl.BlockSpec((None, s, 2 * MLA_V), lambda bi, hp: (bi, 0, hp)),
        out_shape=jax.ShapeDtypeStruct((b, s, MLA_HEADS * MLA_V), BF16),
        scratch_shapes=[pltpu.VMEM((4, ATT_Q, s), F32)],
        compiler_params=_params(2),
    )(q3, k3, v3)


def _out_proj_body(y_ref, x_ref, w_ref, gpost_ref, o_ref):
    o_ref[...] = x_ref[...] + _rms(_dot(y_ref[...], w_ref[...]), gpost_ref[...])


def _out_proj(y2, x2, w, gpost):
    t, kdim = y2.shape
    tm = OUT_TOKENS
    return pl.pallas_call(
        _out_proj_body,
        name="out_proj",
        grid=(t // tm,),
        in_specs=[pl.BlockSpec((tm, kdim), lambda i: (i, 0)),
                  pl.BlockSpec((tm, D_MODEL), lambda i: (i, 0)),
                  _const_spec((kdim, D_MODEL)), _const_spec((1, D_MODEL))],
        out_specs=pl.BlockSpec((tm, D_MODEL), lambda i: (i, 0)),
        out_shape=jax.ShapeDtypeStruct((t, D_MODEL), F32),
        compiler_params=_params(1),
    )(y2, x2, w, gpost)


def _row(v):
    return v.reshape(1, -1).astype(F32)


def _pad_heads(w, width, offset=0):
    k, hh, d = w.shape
    out = jnp.zeros((k, hh, width), w.dtype)
    out = out.at[:, :, offset:offset + d].set(w)
    return out.reshape(k, hh * width)


def _rot_half_cols(w):
    half = w.shape[-1] // 2
    return jnp.concatenate([-w[..., half:], w[..., :half]], axis=-1)


def _mla_layer(x2, pos2, b, s, gpre, gpost, w_in, q_norm, w_uq, kv_norm, w_ukv, w_o):
    o1 = MLA_Q_LORA
    o2 = o1 + MLA_KV_LORA
    w_pe = w_in[:, o2:]
    slab = jnp.zeros((D_MODEL, ATT_PAD), F32).at[:, MLA_NOPE:MLA_NOPE + MLA_ROPE]
    wlat = jnp.concatenate([w_in[:, :o2], slab.set(w_pe), slab.set(_rot_half_cols(w_pe))],
                           axis=1).astype(BF16)
    wq3 = w_uq.reshape(o1, MLA_HEADS, MLA_NOPE + MLA_ROPE)
    wq = _pad_heads(wq3, ATT_PAD).astype(BF16)
    wqr = _pad_heads(_rot_half_cols(wq3[:, :, MLA_NOPE:]), ATT_PAD, MLA_NOPE).astype(BF16)
    wkv3 = w_ukv.reshape(MLA_KV_LORA, MLA_HEADS, MLA_NOPE + MLA_V)
    wk = _pad_heads(wkv3[:, :, :MLA_NOPE], ATT_PAD).astype(BF16)
    wv = wkv3[:, :, MLA_NOPE:].reshape(MLA_KV_LORA, MLA_HEADS * MLA_V).astype(BF16)
    half = MLA_ROPE // 2
    inv = ROPE_BASE ** (-jnp.arange(half, dtype=F32) / half)
    inv_packed = jnp.tile(jnp.concatenate([inv, inv]), ROPE_PACK).reshape(1, ATT_PAD)
    q, k, v = _mla_proj(x2, pos2, _row(gpre), wlat, _row(q_norm), _row(kv_norm), wq, wqr, wk, wv,
                        inv_packed)
    o = _attention(q.reshape(b, s, -1), k.reshape(b, s, -1), v.reshape(b, s, -1))
    return _out_proj(o.reshape(b * s, -1), x2, w_o.astype(BF16), _row(gpost))


def kernel(x, positions, norm_mix_pre, norm_mix_post, norm_ffn_pre, norm_ffn_post, ffn_w_in, ffn_w_out, conv_w_pw1, conv_b_pw1, conv_w_dw, conv_b_dw, conv_ln_g, conv_ln_b, conv_w_pw2, conv_b_pw2, ssm_w_in, ssm_conv_w, ssm_conv_b, ssm_dt_bias, ssm_a_log, ssm_d, ssm_norm_w, ssm_w_out, mla_w_in, mla_q_norm, mla_w_uq, mla_kv_norm, mla_w_ukv, mla_w_o):
    b, s, d = x.shape
    t = b * s
    quarter = MLA_SUB // ROPE_PACK
    pos2 = jnp.repeat(positions.reshape(t // MLA_SUB, ROPE_PACK, quarter).transpose(0, 2, 1),
                      MLA_ROPE, axis=2).reshape(t // ROPE_PACK, ATT_PAD)
    ffn_w1 = ffn_w_in.astype(BF16)
    ffn_w2 = ffn_w_out.astype(BF16)
    conv_w1 = conv_w_pw1.astype(BF16)
    conv_w2 = conv_w_pw2.astype(BF16)
    i_conv = i_ssm = i_mla = 0
    for i in range(DEPTH):
        kind = i % N_MIXERS
        gpre, gpost = _row(norm_mix_pre[i]), _row(norm_mix_post[i])
        if kind == 0:
            jx = i_conv
            x = _conv_layer(
                x.reshape(b, s, d), gpre, gpost, conv_w1, _row(conv_b_pw1[jx]),
                conv_w_dw[jx], _row(conv_b_dw[jx]), _row(conv_ln_g[jx]), _row(conv_ln_b[jx]),
                conv_w2, _row(conv_b_pw2[jx]), jx)
            i_conv += 1
        elif kind == 1:
            jx = i_ssm
            w_in = ssm_w_in[jx].astype(BF16)
            o2 = SSM_D_INNER + SSM_CONV_DIM
            rep = SSM_HEAD_DIM
            pad = LANES - SSM_HEADS
            expand = jnp.repeat(jnp.eye(LANES, SSM_HEADS, dtype=BF16), rep, axis=1)
            x = _ssd_layer(
                x.reshape(b, s, d), gpre, gpost, w_in, jnp.pad(w_in[:, o2:], ((0, 0), (0, pad))),
                ssm_conv_w[jx], _row(ssm_conv_b[jx]), _row(jnp.pad(ssm_dt_bias[jx], (0, pad))),
                expand, _row(jnp.pad(ssm_a_log[jx], (0, pad))),
                _row(jnp.repeat(ssm_a_log[jx], rep)), _row(jnp.repeat(ssm_d[jx], rep)),
                _row(ssm_norm_w[jx]), ssm_w_out[jx].astype(BF16))
            i_ssm += 1
        else:
            jx = i_mla
            x = _mla_layer(x.reshape(t, d), pos2, b, s, norm_mix_pre[i], norm_mix_post[i],
                           mla_w_in[jx], mla_q_norm[jx], mla_w_uq[jx], mla_kv_norm[jx],
                           mla_w_ukv[jx], mla_w_o[jx])
            i_mla += 1
        x = _ffn(x.reshape(t, d), _row(norm_ffn_pre[i]), _row(norm_ffn_post[i]), ffn_w1, ffn_w2, i)
    return x.reshape(b, s, d)
```

```python
import math

import jax
import jax.numpy as jnp
from jax import lax
from jax.experimental import pallas as pl
from jax.experimental.pallas import tpu as pltpu

F32 = jnp.float32
BF16 = jnp.bfloat16

D_MODEL = 1024
DEPTH = 4
CHUNK = 64
N_MIXERS = 3
EPS = 1e-6
FFN_DIM = 4 * D_MODEL
CONV_KERNEL = 31
SSM_D_INNER = 2 * D_MODEL
SSM_HEAD_DIM = 64
SSM_HEADS = SSM_D_INNER // SSM_HEAD_DIM
SSM_GROUPS = 8
SSM_STATE = 128
SSM_CONV = 4
SSM_CONV_DIM = SSM_D_INNER + 2 * SSM_GROUPS * SSM_STATE
MLA_HEADS = D_MODEL // 64
MLA_NOPE = 64
MLA_ROPE = 32
MLA_V = 64
MLA_Q_LORA = 3 * D_MODEL // 8
MLA_KV_LORA = D_MODEL // 4
ROPE_BASE = 10000.0
LOG2E = math.log2(math.e)

LANES = 128
SUBLANES = 8
VMEM_LIMIT_BYTES = 56 * 1024 * 1024

FFN_TOKENS = 2048
FFN_SUB = 512
FFN_CHUNK = 512
CONV_TOKENS = 1024
CONV_HALO = 32
CONV_ROWS = 32
CONV_LN_ROWS = 128
CONV_LANES = 256
SSD_TOKENS = 512
SSD_CHUNK = 128
SSD_HALO = SUBLANES
SSD_CONV_LANES = 512
SSD_AHEAD = 1
MLA_TOKENS = 512
MLA_SUB = 256
ROPE_PACK = 4
ATT_Q = 256
ATT_PAD = 128
ATT_AHEAD = 2
OUT_TOKENS = 1024


def _const_spec(shape):
    nd = len(shape)
    return pl.BlockSpec(shape, lambda *_: (0,) * nd, pipeline_mode=pl.Buffered(1))


def _layer_spec(shape, layer):
    nd = len(shape)
    return pl.BlockSpec((None,) + tuple(shape), lambda *_: (layer,) + (0,) * nd,
                        pipeline_mode=pl.Buffered(1))


def _params(n_axes, parallel=True):
    sem = ("parallel" if parallel else "arbitrary",) * n_axes
    return pltpu.CompilerParams(dimension_semantics=sem, vmem_limit_bytes=VMEM_LIMIT_BYTES)


def _rms(x, g):
    return x * lax.rsqrt(jnp.mean(x * x, axis=-1, keepdims=True) + EPS) * g


def _silu(x):
    half = 0.5 * x
    return half + half * jnp.tanh(half)


def _dot(a, b):
    return jnp.dot(a, b, preferred_element_type=F32)


def _dot_nt(a, b):
    return lax.dot_general(a, b, (((1,), (1,)), ((), ())), preferred_element_type=F32)


def _dot_tn(a, b):
    return lax.dot_general(a, b, (((0,), (0,)), ((), ())), preferred_element_type=F32)


def _split3(x):
    hi = x.astype(BF16)
    r1 = x - hi.astype(F32)
    mid = r1.astype(BF16)
    lo = (r1 - mid.astype(F32)).astype(BF16)
    return hi, mid, lo


def _dot_exact01(m01, x):
    hi, mid, lo = _split3(x)
    return (_dot(jnp.concatenate([m01, m01], axis=1), jnp.concatenate([hi, mid], axis=0))
            + _dot(m01, lo))


def _dot_exact01_rhs(x, m01):
    hi, mid, lo = _split3(x)
    return (_dot(jnp.concatenate([hi, mid], axis=1), jnp.concatenate([m01, m01], axis=0))
            + _dot(lo, m01))


def _ffn_body(x_ref, gpre_ref, gpost_ref, w1_ref, w2_ref, o_ref):
    for sub in range(FFN_TOKENS // FFN_SUB):
        rows = slice(sub * FFN_SUB, (sub + 1) * FFN_SUB)
        x = x_ref[rows, :]
        h = _rms(x, gpre_ref[...]).astype(BF16)
        acc = None
        for c in range(FFN_DIM // FFN_CHUNK):
            cols = slice(c * FFN_CHUNK, (c + 1) * FFN_CHUNK)
            a = _dot(h, w1_ref[:, cols])
            a = jnp.square(jnp.maximum(a, 0.0)).astype(BF16)
            p = _dot(a, w2_ref[cols, :])
            acc = p if acc is None else acc + p
        o_ref[rows, :] = x + _rms(acc, gpost_ref[...])


def _ffn(x2, gpre, gpost, w1_all, w2_all, layer):
    t = x2.shape[0]
    tok = pl.BlockSpec((FFN_TOKENS, D_MODEL), lambda i: (i, 0))
    return pl.pallas_call(
        _ffn_body,
        name="ffn",
        grid=(t // FFN_TOKENS,),
        in_specs=[tok, _const_spec((1, D_MODEL)), _const_spec((1, D_MODEL)),
                  _layer_spec((D_MODEL, FFN_DIM), layer), _layer_spec((FFN_DIM, D_MODEL), layer)],
        out_specs=tok,
        out_shape=jax.ShapeDtypeStruct((t, D_MODEL), F32),
        compiler_params=_params(1),
    )(x2, gpre, gpost, w1_all, w2_all)


def _conv_body(x_ref, gpre_ref, gpost_ref, w1_ref, b1_ref, wdw_ref, bdw_ref,
               lng_ref, lnb_ref, w2_ref, b2_ref, o_ref, ubuf, shifted_ref, dbuf):
    ts = x_ref.shape[0]
    j = pl.program_id(1)

    @pl.when(j == 0)
    def _():
        ubuf[0:CONV_HALO, :] = jnp.zeros((CONV_HALO, D_MODEL), F32)

    @pl.when(j > 0)
    def _():
        ubuf[0:CONV_HALO, :] = ubuf[ts:ts + CONV_HALO, :]

    x = x_ref[...]
    h = _rms(x, gpre_ref[...]).astype(BF16)
    first = CONV_HALO - (CONV_KERNEL - 1)
    n_sh = CONV_HALO + ts - SUBLANES

    for cblk in range(D_MODEL // CONV_LANES):
        shifted = shifted_ref.at[cblk % 2]
        cs = slice(cblk * CONV_LANES, (cblk + 1) * CONV_LANES)
        gs = slice(D_MODEL + cblk * CONV_LANES, D_MODEL + (cblk + 1) * CONV_LANES)
        ua = _dot(h, w1_ref[:, cs]) + b1_ref[:, cs]
        ub = _dot(h, w1_ref[:, gs]) + b1_ref[:, gs]
        ua_half = 0.5 * ua
        ubuf[CONV_HALO:CONV_HALO + ts, cs] = ua_half + ua_half * jnp.tanh(0.5 * ub)
        window = ubuf[:, cs]
        for sft in range(1, SUBLANES):
            rolled = pltpu.roll(window, CONV_HALO + ts - sft, axis=0)
            shifted[sft - 1, 0:n_sh, :] = rolled[0:n_sh, :]
        for sub in range(CONV_LANES // LANES):
            ls = slice(cblk * CONV_LANES + sub * LANES, cblk * CONV_LANES + (sub + 1) * LANES)
            bl = slice(sub * LANES, (sub + 1) * LANES)
            for rb in range(ts // CONV_ROWS):
                acc = jnp.broadcast_to(bdw_ref[:, ls], (CONV_ROWS, LANES))
                for k in range(CONV_KERNEL):
                    lo = rb * CONV_ROWS + first + k
                    sft = lo % SUBLANES
                    if sft:
                        tap = shifted[sft - 1, lo - sft:lo - sft + CONV_ROWS, bl]
                    else:
                        tap = ubuf[lo:lo + CONV_ROWS, ls]
                    acc = acc + wdw_ref[k:k + 1, ls] * tap
                dbuf[rb * CONV_ROWS:(rb + 1) * CONV_ROWS, ls] = acc

    for rt in range(ts // CONV_LN_ROWS):
        rows = slice(rt * CONV_LN_ROWS, (rt + 1) * CONV_LN_ROWS)
        acc = dbuf[rows, :]
        mu = jnp.mean(acc, axis=-1, keepdims=True)
        cen = acc - mu
        var = jnp.mean(cen * cen, axis=-1, keepdims=True)
        y = cen * lax.rsqrt(var + EPS) * lng_ref[...] + lnb_ref[...]
        v = _silu(y).astype(BF16)
        out = _dot(v, w2_ref[...]) + b2_ref[...]
        o_ref[rows, :] = x[rows, :] + _rms(out, gpost_ref[...])


def _conv_layer(x3, gpre, gpost, w1_all, b1, wdw, bdw, lng, lnb, w2_all, b2, layer):
    b, s, _ = x3.shape
    ts = CONV_TOKENS
    tok = pl.BlockSpec((None, ts, D_MODEL), lambda bi, j: (bi, j, 0))
    vec = _const_spec((1, D_MODEL))
    return pl.pallas_call(
        _conv_body,
        name="conv_mixer",
        grid=(b, s // ts),
        in_specs=[tok, vec, vec, _layer_spec((D_MODEL, 2 * D_MODEL), layer),
                  _const_spec((1, 2 * D_MODEL)), _const_spec((CONV_KERNEL, D_MODEL)), vec,
                  vec, vec, _layer_spec((D_MODEL, D_MODEL), layer), vec],
        out_specs=tok,
        out_shape=jax.ShapeDtypeStruct(x3.shape, F32),
        scratch_shapes=[pltpu.VMEM((CONV_HALO + ts, D_MODEL), F32),
                        pltpu.VMEM((2, SUBLANES - 1, CONV_HALO + ts, CONV_LANES), F32),
                        pltpu.VMEM((ts, D_MODEL), F32)],
        compiler_params=_params(2, parallel=False),
    )(x3, gpre, gpost, w1_all, b1, wdw, bdw, lng, lnb, w2_all, b2)


def _ssd_body(x_ref, gpre_ref, gpost_ref, win_ref, wdt_ref, cw_ref, cb_ref, dtb_ref,
              expand_ref, alogh_ref, alog_ref, dskip_ref, nw_ref, wout_ref, o_ref, hbuf, xbuf,
              dtbuf, zbuf, ybuf, state):
    ts = x_ref.shape[0]
    lc = SSD_CHUNK
    gw = SSM_D_INNER // SSM_GROUPS
    j = pl.program_id(1)

    @pl.when(j == 0)
    def _():
        hbuf[...] = jnp.zeros(hbuf.shape, F32)
        state[...] = jnp.zeros(state.shape, F32)

    x = x_ref[...]
    h = _rms(x, gpre_ref[...]).astype(BF16)
    dt_raw = _dot(h, wdt_ref[...]) + dtb_ref[...]
    dt_heads = jnp.maximum(dt_raw, 0.0) + jnp.log1p(jnp.exp(-jnp.abs(dt_raw)))
    dtbuf[...] = _dot_exact01_rhs(dt_heads, expand_ref[...])

    n_cblk = SSM_CONV_DIM // SSD_CONV_LANES

    def project(cblk):
        ws = slice(SSM_D_INNER + cblk * SSD_CONV_LANES, SSM_D_INNER + (cblk + 1) * SSD_CONV_LANES)
        return _dot(h, win_ref[:, ws])

    raw = project(0)
    for cblk in range(n_cblk):
        cs = slice(cblk * SSD_CONV_LANES, (cblk + 1) * SSD_CONV_LANES)
        if cblk + 1 < n_cblk:
            raw_next = project(cblk + 1)
        else:
            raw_next = None
            zbuf[...] = _dot(h, win_ref[:, 0:SSM_D_INNER])
        window = jnp.concatenate([hbuf[:, cs], raw], axis=0)
        hbuf[:, cs] = raw[ts - SSD_HALO:ts, :]
        raw = raw_next
        conv = jnp.broadcast_to(cb_ref[:, cs], (ts, SSD_CONV_LANES))
        for k in range(SSM_CONV):
            back = SSM_CONV - 1 - k
            tap = pltpu.roll(window, back, axis=0) if back else window
            conv = conv + cw_ref[k:k + 1, cs] * tap[SSD_HALO:SSD_HALO + ts, :]
        xbuf[:, cs] = _silu(conv)

    a_row = -jnp.exp(alog_ref[...]) * LOG2E
    a_heads = -jnp.exp(alogh_ref[...]) * LOG2E
    ri = lax.broadcasted_iota(jnp.int32, (lc, lc), 0)
    ci = lax.broadcasted_iota(jnp.int32, (lc, lc), 1)
    causal = ci <= ri
    tril01 = causal.astype(BF16)
    off_b = SSM_D_INNER
    off_c = SSM_D_INNER + SSM_GROUPS * SSM_STATE
    lane = lax.broadcasted_iota(jnp.int32, (1, gw), 1)
    head01 = [((lane >= r * SSM_HEAD_DIM) & (lane < (r + 1) * SSM_HEAD_DIM)).astype(BF16)
              for r in range(gw // SSM_HEAD_DIM)]

    def chunk_terms(c):
        rows = slice(c * lc, (c + 1) * lc)
        dt = dtbuf[rows, :]
        a_cs = _dot_exact01(tril01, dt * a_row)
        a_last = a_cs[lc - 1:lc, :]
        a_csh = _dot_exact01(tril01, dt_heads[rows, :] * a_heads)
        return dict(rows=rows, dt=dt, exp_a=jnp.exp2(a_cs), to_end=jnp.exp2(a_last - a_cs),
                    decay_chunk=jnp.exp2(a_last), a_csh=a_csh, a_csh_t=a_csh.T)

    def group_inputs(ck, g):
        rows = ck["rows"]
        gl = slice(g * gw, (g + 1) * gw)
        xs = xbuf[rows, gl]
        xdt = xs * ck["dt"][:, gl]
        bm = xbuf[rows, off_b + g * SSM_STATE:off_b + (g + 1) * SSM_STATE].astype(BF16)
        cm = xbuf[rows, off_c + g * SSM_STATE:off_c + (g + 1) * SSM_STATE].astype(BF16)
        return xs, xdt, bm, cm, _dot_nt(cm, bm)

    def group_outputs(ck, g, xs, xdt, bm, cm, cb):
        rows = ck["rows"]
        gl = slice(g * gw, (g + 1) * gw)
        xdb = xdt.astype(BF16)
        ms, xds = [], []
        for r in range(gw // SSM_HEAD_DIM):
            hd = g * (gw // SSM_HEAD_DIM) + r
            col = ck["a_csh"][:, hd:hd + 1]
            row = ck["a_csh_t"][hd:hd + 1, :]
            decay = jnp.where(causal, jnp.exp2(col - row), 0.0)
            ms.append((cb * decay).astype(BF16))
            xds.append(xdb * head01[r])
        y = (_dot(jnp.concatenate(ms[0:2], axis=1), jnp.concatenate(xds[0:2], axis=0))
             + _dot(jnp.concatenate(ms[2:4], axis=1), jnp.concatenate(xds[2:4], axis=0)))
        st = state[g]
        y = y + _dot(cm, st.astype(BF16)) * ck["exp_a"][:, gl]
        xw = (xdt * ck["to_end"][:, gl]).astype(BF16)
        state[g] = st * ck["decay_chunk"][:, gl] + _dot_tn(bm, xw)
        y = y + dskip_ref[:, gl] * xs
        y = y * _silu(zbuf[rows, gl])
        y = y * lax.rsqrt(jnp.mean(y * y, axis=-1, keepdims=True) + EPS) * nw_ref[:, gl]
        ybuf[rows, gl] = y.astype(BF16)

    units = [(c, g) for c in range(ts // lc) for g in range(SSM_GROUPS)]
    chunks = {}

    def issue(n):
        c1, g1 = units[n]
        if c1 not in chunks:
            chunks[c1] = chunk_terms(c1)
        return group_inputs(chunks[c1], g1)

    pending = [issue(n) for n in range(SSD_AHEAD)]
    for n, (c, g) in enumerate(units):
        current = pending.pop(0)
        if n + SSD_AHEAD < len(units):
            pending.append(issue(n + SSD_AHEAD))
        group_outputs(chunks[c], g, *current)
        if g == SSM_GROUPS - 1:
            rows = chunks[c]["rows"]
            out = _dot(ybuf[rows, :], wout_ref[...])
            o_ref[rows, :] = x[rows, :] + _rms(out, gpost_ref[...])


def _ssd_layer(x3, gpre, gpost, win, wdt, cw, cb, dtb, expand, alogh, alog, dskip, nw, wout):
    b, s, _ = x3.shape
    ts = SSD_TOKENS
    tok = pl.BlockSpec((None, ts, D_MODEL), lambda bi, j: (bi, j, 0))
    vec = _const_spec((1, D_MODEL))
    inner = _const_spec((1, SSM_D_INNER))
    return pl.pallas_call(
        _ssd_body,
        name="ssd_mixer",
        grid=(b, s // ts),
        in_specs=[tok, vec, vec, _const_spec(win.shape), _const_spec((D_MODEL, LANES)),
                  _const_spec((SSM_CONV, SSM_CONV_DIM)), _const_spec((1, SSM_CONV_DIM)),
                  _const_spec((1, LANES)), _const_spec((LANES, SSM_D_INNER)), _const_spec((1, LANES)),
                  inner, inner, inner, _const_spec((SSM_D_INNER, D_MODEL))],
        out_specs=tok,
        out_shape=jax.ShapeDtypeStruct(x3.shape, F32),
        scratch_shapes=[pltpu.VMEM((SSD_HALO, SSM_CONV_DIM), F32),
                        pltpu.VMEM((ts, SSM_CONV_DIM), F32),
                        pltpu.VMEM((ts, SSM_D_INNER), F32),
                        pltpu.VMEM((ts, SSM_D_INNER), F32),
                        pltpu.VMEM((ts, SSM_D_INNER), BF16),
                        pltpu.VMEM((SSM_GROUPS, SSM_STATE, SSM_D_INNER // SSM_GROUPS), F32)],
        compiler_params=_params(2, parallel=False),
    )(x3, gpre, gpost, win, wdt, cw, cb, dtb, expand, alogh, alog, dskip, nw, wout)


def _mla_proj_body(x_ref, pos_ref, gpre_ref, wlat_ref, qn_ref, kvn_ref, wq_ref, wqr_ref, wk_ref,
                   wv_ref, inv_ref, q_ref, k_ref, v_ref):
    o1 = MLA_Q_LORA
    o2 = o1 + MLA_KV_LORA
    reps = (1, MLA_HEADS)

    def latent(sub):
        rows = slice(sub * MLA_SUB, (sub + 1) * MLA_SUB)
        h = _rms(x_ref[rows, :], gpre_ref[...]).astype(BF16)
        return _dot(h, wlat_ref[...])

    packed = MLA_SUB // ROPE_PACK
    rope_lane = lax.broadcasted_iota(jnp.int32, (packed, ATT_PAD), 1)
    rope_lane = (rope_lane >= MLA_NOPE) & (rope_lane < MLA_NOPE + MLA_ROPE)

    def rope_terms(sub):
        ang = pos_ref[sub * packed:(sub + 1) * packed, :].astype(F32) * inv_ref[...]
        cos_p, sin_p = jnp.cos(ang), jnp.sin(ang)
        cos, sin = [], []
        for q in range(ROPE_PACK):
            shift = (MLA_NOPE - MLA_ROPE * q) % ATT_PAD
            cos_q = pltpu.roll(cos_p, shift, axis=1) if shift else cos_p
            sin_q = pltpu.roll(sin_p, shift, axis=1) if shift else sin_p
            cos.append(jnp.where(rope_lane, cos_q, 1.0))
            sin.append(jnp.where(rope_lane, sin_q, 0.0))
        return jnp.concatenate(cos, axis=0), jnp.concatenate(sin, axis=0)

    n_sub = x_ref.shape[0] // MLA_SUB
    lat_next = latent(0)
    rope_next = rope_terms(0)
    for sub in range(n_sub):
        rows = slice(sub * MLA_SUB, (sub + 1) * MLA_SUB)
        lat, (cos, sin) = lat_next, rope_next
        if sub + 1 < n_sub:
            lat_next = latent(sub + 1)
            rope_next = rope_terms(sub + 1)
        cq = _rms(lat[:, :o1], qn_ref[...]).astype(BF16)
        ckv = _rms(lat[:, o1:o2], kvn_ref[...]).astype(BF16)
        pe = lat[:, o2:o2 + ATT_PAD]
        pe_rot = lat[:, o2 + ATT_PAD:o2 + 2 * ATT_PAD]
        k_pe = pe * cos + pe_rot * sin
        k_ref[rows, :] = (_dot(ckv, wk_ref[...]) + jnp.tile(k_pe, reps)).astype(BF16)
        q = (_dot(cq, wq_ref[...]) * jnp.tile(cos, reps)
             + _dot(cq, wqr_ref[...]) * jnp.tile(sin, reps))
        q_ref[rows, :] = q.astype(BF16)
        v_ref[rows, :] = _dot(ckv, wv_ref[...]).astype(BF16)


def _mla_proj(x2, pos2, gpre, wlat, qn, kvn, wq, wqr, wk, wv, inv):
    t = x2.shape[0]
    tm = MLA_TOKENS
    hp = MLA_HEADS * ATT_PAD
    hv = MLA_HEADS * MLA_V

    def tok(w):
        return pl.BlockSpec((tm, w), lambda i: (i, 0))

    return pl.pallas_call(
        _mla_proj_body,
        name="mla_proj",
        grid=(t // tm,),
        in_specs=[tok(D_MODEL), pl.BlockSpec((tm // ROPE_PACK, ATT_PAD), lambda i: (i, 0)),
                  _const_spec((1, D_MODEL)), _const_spec(wlat.shape),
                  _const_spec((1, MLA_Q_LORA)), _const_spec((1, MLA_KV_LORA)),
                  _const_spec((MLA_Q_LORA, hp)), _const_spec((MLA_Q_LORA, hp)),
                  _const_spec((MLA_KV_LORA, hp)), _const_spec((MLA_KV_LORA, hv)),
                  _const_spec((1, ATT_PAD))],
        out_specs=[tok(hp), tok(hp), tok(hv)],
        out_shape=[jax.ShapeDtypeStruct((t, hp), BF16), jax.ShapeDtypeStruct((t, hp), BF16),
                   jax.ShapeDtypeStruct((t, hv), BF16)],
        compiler_params=_params(1),
    )(x2, pos2, gpre, wlat, qn, kvn, wq, wqr, wk, wv, inv)


def _attn_body(q_ref, k_ref, v_ref, o_ref, s_buf):
    s_len = q_ref.shape[0]
    tq = ATT_Q
    c2 = (MLA_NOPE + MLA_ROPE) ** -0.5 * math.log2(math.e)
    ri = lax.broadcasted_iota(jnp.int32, (tq, tq), 0) // CHUNK
    ci = lax.broadcasted_iota(jnp.int32, (tq, tq), 1) // CHUNK
    diag_mask = ci <= ri
    lane = lax.broadcasted_iota(jnp.int32, (tq, 2 * MLA_V), 1)

    def scores(i, e):
        lo = i * tq
        qrows = slice(lo, lo + tq)
        cols = slice(e * ATT_PAD, (e + 1) * ATT_PAD)
        sb = s_buf.at[2 * (i % 2) + e]
        q = q_ref[qrows, cols]
        sb[:, qrows] = jnp.where(diag_mask, _dot_nt(q, k_ref[qrows, cols]), -jnp.inf)
        if lo:
            sb[:, 0:lo] = _dot_nt(q, k_ref[0:lo, cols])
        return jnp.max(sb[:, 0:lo + tq], axis=-1, keepdims=True)

    def weighted_values(i, e, m):
        hi = (i + 1) * tq
        sb = s_buf.at[2 * (i % 2) + e]
        p = jnp.exp2((sb[:, 0:hi] - m) * c2)
        l = jnp.sum(p, axis=-1, keepdims=True)
        return _dot(p.astype(BF16), v_ref[0:hi, :]) / l

    units = [(i, e) for i in reversed(range(s_len // tq)) for e in range(2)]
    pending = [scores(*u) for u in units[:ATT_AHEAD]]
    outs = {}
    for n, (i, e) in enumerate(units):
        m = pending.pop(0)
        if n + ATT_AHEAD < len(units):
            pending.append(scores(*units[n + ATT_AHEAD]))
        outs[e] = weighted_values(i, e, m)
        if e == 1:
            qrows = slice(i * tq, (i + 1) * tq)
            o_ref[qrows, :] = jnp.where(lane < MLA_V, outs[0], outs[1]).astype(BF16)


def _attention(q3, k3, v3):
    b, s, _ = q3.shape
    return pl.pallas_call(
        _attn_body,
        name="mla_attn",
        grid=(b, MLA_HEADS // 2),
        in_specs=[pl.BlockSpec((None, s, 2 * ATT_PAD), lambda bi, hp: (bi, 0, hp)),
                  pl.BlockSpec((None, s, 2 * ATT_PAD), lambda bi, hp: (bi, 0, hp)),
                  pl.BlockSpec((None, s, 2 * MLA_V), lambda bi, hp: (bi, 0, hp))],
        out_specs=pl.BlockSpec((None, s, 2 * MLA_V), lambda bi, hp: (bi, 0, hp)),
        out_shape=jax.ShapeDtypeStruct((b, s, MLA_HEADS * MLA_V), BF16),
        scratch_shapes=[pltpu.VMEM((4, ATT_Q, s), F32)],
        compiler_params=_params(2),
    )(q3, k3, v3)


def _out_proj_body(y_ref, x_ref, w_ref, gpost_ref, o_ref):
    o_ref[...] = x_ref[...] + _rms(_dot(y_ref[...], w_ref[...]), gpost_ref[...])


def _out_proj(y2, x2, w, gpost):
    t, kdim = y2.shape
    tm = OUT_TOKENS
    return pl.pallas_call(
        _out_proj_body,
        name="out_proj",
        grid=(t // tm,),
        in_specs=[pl.BlockSpec((tm, kdim), lambda i: (i, 0)),
                  pl.BlockSpec((tm, D_MODEL), lambda i: (i, 0)),
                  _const_spec((kdim, D_MODEL)), _const_spec((1, D_MODEL))],
        out_specs=pl.BlockSpec((tm, D_MODEL), lambda i: (i, 0)),
        out_shape=jax.ShapeDtypeStruct((t, D_MODEL), F32),
        compiler_params=_params(1),
    )(y2, x2, w, gpost)


def _row(v):
    return v.reshape(1, -1).astype(F32)


def _pad_heads(w, width, offset=0):
    k, hh, d = w.shape
    out = jnp.zeros((k, hh, width), w.dtype)
    out = out.at[:, :, offset:offset + d].set(w)
    return out.reshape(k, hh * width)


def _rot_half_cols(w):
    half = w.shape[-1] // 2
    return jnp.concatenate([-w[..., half:], w[..., :half]], axis=-1)


def _mla_layer(x2, pos2, b, s, gpre, gpost, w_in, q_norm, w_uq, kv_norm, w_ukv, w_o):
    o1 = MLA_Q_LORA
    o2 = o1 + MLA_KV_LORA
    w_pe = w_in[:, o2:]
    slab = jnp.zeros((D_MODEL, ATT_PAD), F32).at[:, MLA_NOPE:MLA_NOPE + MLA_ROPE]
    wlat = jnp.concatenate([w_in[:, :o2], slab.set(w_pe), slab.set(_rot_half_cols(w_pe))],
                           axis=1).astype(BF16)
    wq3 = w_uq.reshape(o1, MLA_HEADS, MLA_NOPE + MLA_ROPE)
    wq = _pad_heads(wq3, ATT_PAD).astype(BF16)
    wqr = _pad_heads(_rot_half_cols(wq3[:, :, MLA_NOPE:]), ATT_PAD, MLA_NOPE).astype(BF16)
    wkv3 = w_ukv.reshape(MLA_KV_LORA, MLA_HEADS, MLA_NOPE + MLA_V)
    wk = _pad_heads(wkv3[:, :, :MLA_NOPE], ATT_PAD).astype(BF16)
    wv = wkv3[:, :, MLA_NOPE:].reshape(MLA_KV_LORA, MLA_HEADS * MLA_V).astype(BF16)
    half = MLA_ROPE // 2
    inv = ROPE_BASE ** (-jnp.arange(half, dtype=F32) / half)
    inv_packed = jnp.tile(jnp.concatenate([inv, inv]), ROPE_PACK).reshape(1, ATT_PAD)
    q, k, v = _mla_proj(x2, pos2, _row(gpre), wlat, _row(q_norm), _row(kv_norm), wq, wqr, wk, wv,
                        inv_packed)
    o = _attention(q.reshape(b, s, -1), k.reshape(b, s, -1), v.reshape(b, s, -1))
    return _out_proj(o.reshape(b * s, -1), x2, w_o.astype(BF16), _row(gpost))


def kernel(x, positions, norm_mix_pre, norm_mix_post, norm_ffn_pre, norm_ffn_post, ffn_w_in, ffn_w_out, conv_w_pw1, conv_b_pw1, conv_w_dw, conv_b_dw, conv_ln_g, conv_ln_b, conv_w_pw2, conv_b_pw2, ssm_w_in, ssm_conv_w, ssm_conv_b, ssm_dt_bias, ssm_a_log, ssm_d, ssm_norm_w, ssm_w_out, mla_w_in, mla_q_norm, mla_w_uq, mla_kv_norm, mla_w_ukv, mla_w_o):
    b, s, d = x.shape
    t = b * s
    quarter = MLA_SUB // ROPE_PACK
    pos2 = jnp.repeat(positions.reshape(t // MLA_SUB, ROPE_PACK, quarter).transpose(0, 2, 1),
                      MLA_ROPE, axis=2).reshape(t // ROPE_PACK, ATT_PAD)
    ffn_w1 = ffn_w_in.astype(BF16)
    ffn_w2 = ffn_w_out.astype(BF16)
    conv_w1 = conv_w_pw1.astype(BF16)
    conv_w2 = conv_w_pw2.astype(BF16)
    i_conv = i_ssm = i_mla = 0
    for i in range(DEPTH):
        kind = i % N_MIXERS
        gpre, gpost = _row(norm_mix_pre[i]), _row(norm_mix_post[i])
        if kind == 0:
            jx = i_conv
            x = _conv_layer(
                x.reshape(b, s, d), gpre, gpost, conv_w1, _row(conv_b_pw1[jx]),
                conv_w_dw[jx], _row(conv_b_dw[jx]), _row(conv_ln_g[jx]), _row(conv_ln_b[jx]),
                conv_w2, _row(conv_b_pw2[jx]), jx)
            i_conv += 1
        elif kind == 1:
            jx = i_ssm
            w_in = ssm_w_in[jx].astype(BF16)
            o2 = SSM_D_INNER + SSM_CONV_DIM
            rep = SSM_HEAD_DIM
            pad = LANES - SSM_HEADS
            expand = jnp.repeat(jnp.eye(LANES, SSM_HEADS, dtype=BF16), rep, axis=1)
            x = _ssd_layer(
                x.reshape(b, s, d), gpre, gpost, w_in, jnp.pad(w_in[:, o2:], ((0, 0), (0, pad))),
                ssm_conv_w[jx], _row(ssm_conv_b[jx]), _row(jnp.pad(ssm_dt_bias[jx], (0, pad))),
                expand, _row(jnp.pad(ssm_a_log[jx], (0, pad))),
                _row(jnp.repeat(ssm_a_log[jx], rep)), _row(jnp.repeat(ssm_d[jx], rep)),
                _row(ssm_norm_w[jx]), ssm_w_out[jx].astype(BF16))
            i_ssm += 1
        else:
            jx = i_mla
            x = _mla_layer(x.reshape(t, d), pos2, b, s, norm_mix_pre[i], norm_mix_post[i],
                           mla_w_in[jx], mla_q_norm[jx], mla_w_uq[jx], mla_kv_norm[jx],
                           mla_w_ukv[jx], mla_w_o[jx])
            i_mla += 1
        x = _ffn(x.reshape(t, d), _row(norm_ffn_pre[i]), _row(norm_ffn_post[i]), ffn_w1, ffn_w2, i)
    return x.reshape(b, s, d)
```

```python
import functools
import math

import jax
import jax.numpy as jnp
from jax import lax
from jax.experimental import pallas as pl
from jax.experimental.pallas import tpu as pltpu

F32 = jnp.float32
BF16 = jnp.bfloat16

D_MODEL = 1024
DEPTH = 4
CHUNK = 64
N_MIXERS = 3
EPS = 1e-6
FFN_DIM = 4 * D_MODEL
CONV_KERNEL = 31
SSM_D_INNER = 2 * D_MODEL
SSM_HEAD_DIM = 64
SSM_HEADS = SSM_D_INNER // SSM_HEAD_DIM
SSM_GROUPS = 8
SSM_STATE = 128
SSM_CONV = 4
SSM_CONV_DIM = SSM_D_INNER + 2 * SSM_GROUPS * SSM_STATE
MLA_HEADS = D_MODEL // 64
MLA_NOPE = 64
MLA_ROPE = 32
MLA_V = 64
MLA_Q_LORA = 3 * D_MODEL // 8
MLA_KV_LORA = D_MODEL // 4
ROPE_BASE = 10000.0
LOG2E = math.log2(math.e)

LANES = 128
SUBLANES = 8
VMEM_LIMIT_BYTES = 56 * 1024 * 1024

FFN_TOKENS = 1024
FFN_SUB = 512
FFN_CHUNK = 512
CONV_TOKENS = 1024
CONV_HALO = 32
CONV_ROWS = 32
CONV_LN_ROWS = 128
CONV_LANES = 256
SSD_TOKENS = 512
SSD_CHUNK = 128
SSD_HALO = SUBLANES
SSD_CONV_LANES = 512
SSD_AHEAD = 1
MLA_TOKENS = 512
MLA_SUB = 256
ROPE_PACK = 4
ATT_Q = 256
ATT_PAD = 128
ATT_AHEAD = 2
OUT_TOKENS = 1024


def _const_spec(shape):
    nd = len(shape)
    return pl.BlockSpec(shape, lambda *_: (0,) * nd, pipeline_mode=pl.Buffered(1))


def _layer_spec(shape, layer):
    nd = len(shape)
    return pl.BlockSpec((None,) + tuple(shape), lambda *_: (layer,) + (0,) * nd,
                        pipeline_mode=pl.Buffered(1))


def _params(n_axes, parallel=True):
    sem = ("parallel" if parallel else "arbitrary",) * n_axes
    return pltpu.CompilerParams(dimension_semantics=sem, vmem_limit_bytes=VMEM_LIMIT_BYTES)


def _rms(x, g):
    return x * lax.rsqrt(jnp.mean(x * x, axis=-1, keepdims=True) + EPS) * g


def _silu(x):
    half = 0.5 * x
    return half + half * jnp.tanh(half)


def _dot(a, b):
    return jnp.dot(a, b, preferred_element_type=F32)


def _dot_nt(a, b):
    return lax.dot_general(a, b, (((1,), (1,)), ((), ())), preferred_element_type=F32)


def _dot_tn(a, b):
    return lax.dot_general(a, b, (((0,), (0,)), ((), ())), preferred_element_type=F32)


def _split3(x):
    hi = x.astype(BF16)
    r1 = x - hi.astype(F32)
    mid = r1.astype(BF16)
    lo = (r1 - mid.astype(F32)).astype(BF16)
    return hi, mid, lo


def _dot_exact01(m01, x):
    hi, mid, lo = _split3(x)
    return (_dot(jnp.concatenate([m01, m01], axis=1), jnp.concatenate([hi, mid], axis=0))
            + _dot(m01, lo))


def _dot_exact01_rhs(x, m01):
    hi, mid, lo = _split3(x)
    return (_dot(jnp.concatenate([hi, mid], axis=1), jnp.concatenate([m01, m01], axis=0))
            + _dot(lo, m01))


def _ffn_body(*refs, mixer_out):
    if mixer_out:
        y_ref, wmix_ref, gmix_ref, x_ref, gpre_ref, gpost_ref, w1_ref, w2_ref, o_ref = refs
    else:
        x_ref, gpre_ref, gpost_ref, w1_ref, w2_ref, o_ref = refs
    for sub in range(FFN_TOKENS // FFN_SUB):
        rows = slice(sub * FFN_SUB, (sub + 1) * FFN_SUB)
        x = x_ref[rows, :]
        if mixer_out:
            x = x + _rms(_dot(y_ref[rows, :], wmix_ref[...]), gmix_ref[...])
        h = _rms(x, gpre_ref[...]).astype(BF16)
        acc = None
        for c in range(FFN_DIM // FFN_CHUNK):
            cols = slice(c * FFN_CHUNK, (c + 1) * FFN_CHUNK)
            a = _dot(h, w1_ref[:, cols])
            a = jnp.square(jnp.maximum(a, 0.0)).astype(BF16)
            p = _dot(a, w2_ref[cols, :])
            acc = p if acc is None else acc + p
        o_ref[rows, :] = x + _rms(acc, gpost_ref[...])


def _ffn(x2, gpre, gpost, w1_all, w2_all, layer, mixer_out=None):
    t = x2.shape[0]
    tok = pl.BlockSpec((FFN_TOKENS, D_MODEL), lambda i: (i, 0))
    vec = _const_spec((1, D_MODEL))
    specs = [tok, vec, vec,
             _layer_spec((D_MODEL, FFN_DIM), layer), _layer_spec((FFN_DIM, D_MODEL), layer)]
    args = (x2, gpre, gpost, w1_all, w2_all)
    if mixer_out is not None:
        y2, w_mix, g_mix = mixer_out
        specs = [pl.BlockSpec((FFN_TOKENS, y2.shape[1]), lambda i: (i, 0)),
                 _const_spec(w_mix.shape), vec] + specs
        args = (y2, w_mix, g_mix) + args
    return pl.pallas_call(
        functools.partial(_ffn_body, mixer_out=mixer_out is not None),
        name="ffn",
        grid=(t // FFN_TOKENS,),
        in_specs=specs,
        out_specs=tok,
        out_shape=jax.ShapeDtypeStruct((t, D_MODEL), F32),
        compiler_params=_params(1),
    )(*args)


def _conv_body(x_ref, gpre_ref, gpost_ref, w1_ref, b1_ref, wdw_ref, bdw_ref,
               lng_ref, lnb_ref, w2_ref, b2_ref, o_ref, ubuf, shifted_ref, dbuf):
    ts = x_ref.shape[0]
    j = pl.program_id(1)

    @pl.when(j == 0)
    def _():
        ubuf[0:CONV_HALO, :] = jnp.zeros((CONV_HALO, D_MODEL), F32)

    @pl.when(j > 0)
    def _():
        ubuf[0:CONV_HALO, :] = ubuf[ts:ts + CONV_HALO, :]

    x = x_ref[...]
    h = _rms(x, gpre_ref[...]).astype(BF16)
    first = CONV_HALO - (CONV_KERNEL - 1)
    n_sh = CONV_HALO + ts - SUBLANES

    for cblk in range(D_MODEL // CONV_LANES):
        shifted = shifted_ref.at[cblk % 2]
        cs = slice(cblk * CONV_LANES, (cblk + 1) * CONV_LANES)
        gs = slice(D_MODEL + cblk * CONV_LANES, D_MODEL + (cblk + 1) * CONV_LANES)
        ua = _dot(h, w1_ref[:, cs]) + b1_ref[:, cs]
        ub = _dot(h, w1_ref[:, gs]) + b1_ref[:, gs]
        ua_half = 0.5 * ua
        ubuf[CONV_HALO:CONV_HALO + ts, cs] = ua_half + ua_half * jnp.tanh(0.5 * ub)
        window = ubuf[:, cs]
        for sft in range(1, SUBLANES):
            rolled = pltpu.roll(window, CONV_HALO + ts - sft, axis=0)
            shifted[sft - 1, 0:n_sh, :] = rolled[0:n_sh, :]
        for sub in range(CONV_LANES // LANES):
            ls = slice(cblk * CONV_LANES + sub * LANES, cblk * CONV_LANES + (sub + 1) * LANES)
            bl = slice(sub * LANES, (sub + 1) * LANES)
            for rb in range(ts // CONV_ROWS):
                acc = jnp.broadcast_to(bdw_ref[:, ls], (CONV_ROWS, LANES))
                for k in range(CONV_KERNEL):
                    lo = rb * CONV_ROWS + first + k
                    sft = lo % SUBLANES
                    if sft:
                        tap = shifted[sft - 1, lo - sft:lo - sft + CONV_ROWS, bl]
                    else:
                        tap = ubuf[lo:lo + CONV_ROWS, ls]
                    acc = acc + wdw_ref[k:k + 1, ls] * tap
                dbuf[rb * CONV_ROWS:(rb + 1) * CONV_ROWS, ls] = acc

    for rt in range(ts // CONV_LN_ROWS):
        rows = slice(rt * CONV_LN_ROWS, (rt + 1) * CONV_LN_ROWS)
        acc = dbuf[rows, :]
        mu = jnp.mean(acc, axis=-1, keepdims=True)
        cen = acc - mu
        var = jnp.mean(cen * cen, axis=-1, keepdims=True)
        y = cen * lax.rsqrt(var + EPS) * lng_ref[...] + lnb_ref[...]
        v = _silu(y).astype(BF16)
        out = _dot(v, w2_ref[...]) + b2_ref[...]
        o_ref[rows, :] = x[rows, :] + _rms(out, gpost_ref[...])


def _conv_layer(x3, gpre, gpost, w1_all, b1, wdw, bdw, lng, lnb, w2_all, b2, layer):
    b, s, _ = x3.shape
    ts = CONV_TOKENS
    tok = pl.BlockSpec((None, ts, D_MODEL), lambda bi, j: (bi, j, 0))
    vec = _const_spec((1, D_MODEL))
    return pl.pallas_call(
        _conv_body,
        name="conv_mixer",
        grid=(b, s // ts),
        in_specs=[tok, vec, vec, _layer_spec((D_MODEL, 2 * D_MODEL), layer),
                  _const_spec((1, 2 * D_MODEL)), _const_spec((CONV_KERNEL, D_MODEL)), vec,
                  vec, vec, _layer_spec((D_MODEL, D_MODEL), layer), vec],
        out_specs=tok,
        out_shape=jax.ShapeDtypeStruct(x3.shape, F32),
        scratch_shapes=[pltpu.VMEM((CONV_HALO + ts, D_MODEL), F32),
                        pltpu.VMEM((2, SUBLANES - 1, CONV_HALO + ts, CONV_LANES), F32),
                        pltpu.VMEM((ts, D_MODEL), F32)],
        compiler_params=_params(2, parallel=False),
    )(x3, gpre, gpost, w1_all, b1, wdw, bdw, lng, lnb, w2_all, b2)


def _ssd_body(x_ref, gpre_ref, gpost_ref, win_ref, wdt_ref, cw_ref, cb_ref, dtb_ref,
              expand_ref, alogh_ref, alog_ref, dskip_ref, nw_ref, wout_ref, o_ref, hbuf, xbuf,
              dtbuf, zbuf, ybuf, state):
    ts = x_ref.shape[0]
    lc = SSD_CHUNK
    gw = SSM_D_INNER // SSM_GROUPS
    j = pl.program_id(1)

    @pl.when(j == 0)
    def _():
        hbuf[...] = jnp.zeros(hbuf.shape, F32)
        state[...] = jnp.zeros(state.shape, F32)

    x = x_ref[...]
    h = _rms(x, gpre_ref[...]).astype(BF16)
    dt_raw = _dot(h, wdt_ref[...]) + dtb_ref[...]
    dt_heads = jnp.maximum(dt_raw, 0.0) + jnp.log1p(jnp.exp(-jnp.abs(dt_raw)))
    dtbuf[...] = _dot_exact01_rhs(dt_heads, expand_ref[...])

    n_cblk = SSM_CONV_DIM // SSD_CONV_LANES

    def project(cblk):
        ws = slice(SSM_D_INNER + cblk * SSD_CONV_LANES, SSM_D_INNER + (cblk + 1) * SSD_CONV_LANES)
        return _dot(h, win_ref[:, ws])

    raw = project(0)
    for cblk in range(n_cblk):
        cs = slice(cblk * SSD_CONV_LANES, (cblk + 1) * SSD_CONV_LANES)
        if cblk + 1 < n_cblk:
            raw_next = project(cblk + 1)
        else:
            raw_next = None
            zbuf[...] = _dot(h, win_ref[:, 0:SSM_D_INNER])
        window = jnp.concatenate([hbuf[:, cs], raw], axis=0)
        hbuf[:, cs] = raw[ts - SSD_HALO:ts, :]
        raw = raw_next
        conv = jnp.broadcast_to(cb_ref[:, cs], (ts, SSD_CONV_LANES))
        for k in range(SSM_CONV):
            back = SSM_CONV - 1 - k
            tap = pltpu.roll(window, back, axis=0) if back else window
            conv = conv + cw_ref[k:k + 1, cs] * tap[SSD_HALO:SSD_HALO + ts, :]
        xbuf[:, cs] = _silu(conv)

    a_row = -jnp.exp(alog_ref[...]) * LOG2E
    a_heads = -jnp.exp(alogh_ref[...]) * LOG2E
    ri = lax.broadcasted_iota(jnp.int32, (lc, lc), 0)
    ci = lax.broadcasted_iota(jnp.int32, (lc, lc), 1)
    causal = ci <= ri
    tril01 = causal.astype(BF16)
    off_b = SSM_D_INNER
    off_c = SSM_D_INNER + SSM_GROUPS * SSM_STATE
    lane = lax.broadcasted_iota(jnp.int32, (1, gw), 1)
    head01 = [((lane >= r * SSM_HEAD_DIM) & (lane < (r + 1) * SSM_HEAD_DIM)).astype(BF16)
              for r in range(gw // SSM_HEAD_DIM)]

    def chunk_terms(c):
        rows = slice(c * lc, (c + 1) * lc)
        dt = dtbuf[rows, :]
        a_cs = _dot_exact01(tril01, dt * a_row)
        a_last = a_cs[lc - 1:lc, :]
        a_csh = _dot_exact01(tril01, dt_heads[rows, :] * a_heads)
        return dict(rows=rows, dt=dt, exp_a=jnp.exp2(a_cs), to_end=jnp.exp2(a_last - a_cs),
                    decay_chunk=jnp.exp2(a_last), a_csh=a_csh, a_csh_t=a_csh.T)

    def group_inputs(ck, g):
        rows = ck["rows"]
        gl = slice(g * gw, (g + 1) * gw)
        xs = xbuf[rows, gl]
        xdt = xs * ck["dt"][:, gl]
        bm = xbuf[rows, off_b + g * SSM_STATE:off_b + (g + 1) * SSM_STATE].astype(BF16)
        cm = xbuf[rows, off_c + g * SSM_STATE:off_c + (g + 1) * SSM_STATE].astype(BF16)
        return xs, xdt, bm, cm, _dot_nt(cm, bm)

    def group_outputs(ck, g, xs, xdt, bm, cm, cb):
        rows = ck["rows"]
        gl = slice(g * gw, (g + 1) * gw)
        xdb = xdt.astype(BF16)
        ms, xds = [], []
        for r in range(gw // SSM_HEAD_DIM):
            hd = g * (gw // SSM_HEAD_DIM) + r
            col = ck["a_csh"][:, hd:hd + 1]
            row = ck["a_csh_t"][hd:hd + 1, :]
            decay = jnp.where(causal, jnp.exp2(col - row), 0.0)
            ms.append((cb * decay).astype(BF16))
            xds.append(xdb * head01[r])
        y = (_dot(jnp.concatenate(ms[0:2], axis=1), jnp.concatenate(xds[0:2], axis=0))
             + _dot(jnp.concatenate(ms[2:4], axis=1), jnp.concatenate(xds[2:4], axis=0)))
        st = state[g]
        y = y + _dot(cm, st.astype(BF16)) * ck["exp_a"][:, gl]
        xw = (xdt * ck["to_end"][:, gl]).astype(BF16)
        state[g] = st * ck["decay_chunk"][:, gl] + _dot_tn(bm, xw)
        y = y + dskip_ref[:, gl] * xs
        y = y * _silu(zbuf[rows, gl])
        y = y * lax.rsqrt(jnp.mean(y * y, axis=-1, keepdims=True) + EPS) * nw_ref[:, gl]
        ybuf[rows, gl] = y.astype(BF16)

    units = [(c, g) for c in range(ts // lc) for g in range(SSM_GROUPS)]
    chunks = {}

    def issue(n):
        c1, g1 = units[n]
        if c1 not in chunks:
            chunks[c1] = chunk_terms(c1)
        return group_inputs(chunks[c1], g1)

    pending = [issue(n) for n in range(SSD_AHEAD)]
    for n, (c, g) in enumerate(units):
        current = pending.pop(0)
        if n + SSD_AHEAD < len(units):
            pending.append(issue(n + SSD_AHEAD))
        group_outputs(chunks[c], g, *current)
        if g == SSM_GROUPS - 1:
            rows = chunks[c]["rows"]
            out = _dot(ybuf[rows, :], wout_ref[...])
            o_ref[rows, :] = x[rows, :] + _rms(out, gpost_ref[...])


def _ssd_layer(x3, gpre, gpost, win, wdt, cw, cb, dtb, expand, alogh, alog, dskip, nw, wout):
    b, s, _ = x3.shape
    ts = SSD_TOKENS
    tok = pl.BlockSpec((None, ts, D_MODEL), lambda bi, j: (bi, j, 0))
    vec = _const_spec((1, D_MODEL))
    inner = _const_spec((1, SSM_D_INNER))
    return pl.pallas_call(
        _ssd_body,
        name="ssd_mixer",
        grid=(b, s // ts),
        in_specs=[tok, vec, vec, _const_spec(win.shape), _const_spec((D_MODEL, LANES)),
                  _const_spec((SSM_CONV, SSM_CONV_DIM)), _const_spec((1, SSM_CONV_DIM)),
                  _const_spec((1, LANES)), _const_spec((LANES, SSM_D_INNER)), _const_spec((1, LANES)),
                  inner, inner, inner, _const_spec((SSM_D_INNER, D_MODEL))],
        out_specs=tok,
        out_shape=jax.ShapeDtypeStruct(x3.shape, F32),
        scratch_shapes=[pltpu.VMEM((SSD_HALO, SSM_CONV_DIM), F32),
                        pltpu.VMEM((ts, SSM_CONV_DIM), F32),
                        pltpu.VMEM((ts, SSM_D_INNER), F32),
                        pltpu.VMEM((ts, SSM_D_INNER), F32),
                        pltpu.VMEM((ts, SSM_D_INNER), BF16),
                        pltpu.VMEM((SSM_GROUPS, SSM_STATE, SSM_D_INNER // SSM_GROUPS), F32)],
        compiler_params=_params(2, parallel=False),
    )(x3, gpre, gpost, win, wdt, cw, cb, dtb, expand, alogh, alog, dskip, nw, wout)


def _mla_proj_body(x_ref, pos_ref, gpre_ref, wlat_ref, qn_ref, kvn_ref, wq_ref, wqr_ref, wk_ref,
                   wv_ref, inv_ref, q_ref, k_ref, v_ref):
    o1 = MLA_Q_LORA
    o2 = o1 + MLA_KV_LORA
    reps = (1, MLA_HEADS)

    def latent(sub):
        rows = slice(sub * MLA_SUB, (sub + 1) * MLA_SUB)
        h = _rms(x_ref[rows, :], gpre_ref[...]).astype(BF16)
        return _dot(h, wlat_ref[...])

    packed = MLA_SUB // ROPE_PACK
    rope_lane = lax.broadcasted_iota(jnp.int32, (packed, ATT_PAD), 1)
    rope_lane = (rope_lane >= MLA_NOPE) & (rope_lane < MLA_NOPE + MLA_ROPE)

    def rope_terms(sub):
        ang = pos_ref[sub * packed:(sub + 1) * packed, :].astype(F32) * inv_ref[...]
        cos_p, sin_p = jnp.cos(ang), jnp.sin(ang)
        cos, sin = [], []
        for q in range(ROPE_PACK):
            shift = (MLA_NOPE - MLA_ROPE * q) % ATT_PAD
            cos_q = pltpu.roll(cos_p, shift, axis=1) if shift else cos_p
            sin_q = pltpu.roll(sin_p, shift, axis=1) if shift else sin_p
            cos.append(jnp.where(rope_lane, cos_q, 1.0))
            sin.append(jnp.where(rope_lane, sin_q, 0.0))
        return jnp.concatenate(cos, axis=0), jnp.concatenate(sin, axis=0)

    n_sub = x_ref.shape[0] // MLA_SUB
    lat_next = latent(0)
    rope_next = rope_terms(0)
    for sub in range(n_sub):
        rows = slice(sub * MLA_SUB, (sub + 1) * MLA_SUB)
        lat, (cos, sin) = lat_next, rope_next
        if sub + 1 < n_sub:
            lat_next = latent(sub + 1)
            rope_next = rope_terms(sub + 1)
        cq = _rms(lat[:, :o1], qn_ref[...]).astype(BF16)
        ckv = _rms(lat[:, o1:o2], kvn_ref[...]).astype(BF16)
        pe = lat[:, o2:o2 + ATT_PAD]
        pe_rot = lat[:, o2 + ATT_PAD:o2 + 2 * ATT_PAD]
        k_pe = pe * cos + pe_rot * sin
        k_ref[rows, :] = (_dot(ckv, wk_ref[...]) + jnp.tile(k_pe, reps)).astype(BF16)
        q = (_dot(cq, wq_ref[...]) * jnp.tile(cos, reps)
             + _dot(cq, wqr_ref[...]) * jnp.tile(sin, reps))
        q_ref[rows, :] = q.astype(BF16)
        v_ref[rows, :] = _dot(ckv, wv_ref[...]).astype(BF16)


def _mla_proj(x2, pos2, gpre, wlat, qn, kvn, wq, wqr, wk, wv, inv):
    t = x2.shape[0]
    tm = MLA_TOKENS
    hp = MLA_HEADS * ATT_PAD
    hv = MLA_HEADS * MLA_V

    def tok(w):
        return pl.BlockSpec((tm, w), lambda i: (i, 0))

    return pl.pallas_call(
        _mla_proj_body,
        name="mla_proj",
        grid=(t // tm,),
        in_specs=[tok(D_MODEL), pl.BlockSpec((tm // ROPE_PACK, ATT_PAD), lambda i: (i, 0)),
                  _const_spec((1, D_MODEL)), _const_spec(wlat.shape),
                  _const_spec((1, MLA_Q_LORA)), _const_spec((1, MLA_KV_LORA)),
                  _const_spec((MLA_Q_LORA, hp)), _const_spec((MLA_Q_LORA, hp)),
                  _const_spec((MLA_KV_LORA, hp)), _const_spec((MLA_KV_LORA, hv)),
                  _const_spec((1, ATT_PAD))],
        out_specs=[tok(hp), tok(hp), tok(hv)],
        out_shape=[jax.ShapeDtypeStruct((t, hp), BF16), jax.ShapeDtypeStruct((t, hp), BF16),
                   jax.ShapeDtypeStruct((t, hv), BF16)],
        compiler_params=_params(1),
    )(x2, pos2, gpre, wlat, qn, kvn, wq, wqr, wk, wv, inv)


def _attn_body(q_ref, k_ref, v_ref, o_ref, s_buf):
    s_len = q_ref.shape[0]
    tq = ATT_Q
    c2 = (MLA_NOPE + MLA_ROPE) ** -0.5 * math.log2(math.e)
    ri = lax.broadcasted_iota(jnp.int32, (tq, tq), 0) // CHUNK
    ci = lax.broadcasted_iota(jnp.int32, (tq, tq), 1) // CHUNK
    diag_mask = ci <= ri
    lane = lax.broadcasted_iota(jnp.int32, (tq, 2 * MLA_V), 1)

    def scores(i, e):
        lo = i * tq
        qrows = slice(lo, lo + tq)
        cols = slice(e * ATT_PAD, (e + 1) * ATT_PAD)
        sb = s_buf.at[2 * (i % 2) + e]
        q = q_ref[qrows, cols]
        sb[:, qrows] = jnp.where(diag_mask, _dot_nt(q, k_ref[qrows, cols]), -jnp.inf)
        if lo:
            sb[:, 0:lo] = _dot_nt(q, k_ref[0:lo, cols])
        return jnp.max(sb[:, 0:lo + tq], axis=-1, keepdims=True)

    def weighted_values(i, e, m):
        hi = (i + 1) * tq
        sb = s_buf.at[2 * (i % 2) + e]
        p = jnp.exp2((sb[:, 0:hi] - m) * c2)
        l = jnp.sum(p, axis=-1, keepdims=True)
        return _dot(p.astype(BF16), v_ref[0:hi, :]) / l

    units = [(i, e) for i in reversed(range(s_len // tq)) for e in range(2)]
    pending = [scores(*u) for u in units[:ATT_AHEAD]]
    outs = {}
    for n, (i, e) in enumerate(units):
        m = pending.pop(0)
        if n + ATT_AHEAD < len(units):
            pending.append(scores(*units[n + ATT_AHEAD]))
        outs[e] = weighted_values(i, e, m)
        if e == 1:
            qrows = slice(i * tq, (i + 1) * tq)
            o_ref[qrows, :] = jnp.where(lane < MLA_V, outs[0], outs[1]).astype(BF16)


def _attention(q3, k3, v3):
    b, s, _ = q3.shape
    return pl.pallas_call(
        _attn_body,
        name="mla_attn",
        grid=(b, MLA_HEADS // 2),
        in_specs=[pl.BlockSpec((None, s, 2 * ATT_PAD), lambda bi, hp: (bi, 0, hp)),
                  pl.BlockSpec((None, s, 2 * ATT_PAD), lambda bi, hp: (bi, 0, hp)),
                  pl.BlockSpec((None, s, 2 * MLA_V), lambda bi, hp: (bi, 0, hp))],
        out_specs=pl.BlockSpec((None, s, 2 * MLA_V), lambda bi, hp: (bi, 0, hp)),
        out_shape=jax.ShapeDtypeStruct((b, s, MLA_HEADS * MLA_V), BF16),
        scratch_shapes=[pltpu.VMEM((4, ATT_Q, s), F32)],
        compiler_params=_params(2),
    )(q3, k3, v3)


def _out_proj_body(y_ref, x_ref, w_ref, gpost_ref, o_ref):
    o_ref[...] = x_ref[...] + _rms(_dot(y_ref[...], w_ref[...]), gpost_ref[...])


def _out_proj(y2, x2, w, gpost):
    t, kdim = y2.shape
    tm = OUT_TOKENS
    return pl.pallas_call(
        _out_proj_body,
        name="out_proj",
        grid=(t // tm,),
        in_specs=[pl.BlockSpec((tm, kdim), lambda i: (i, 0)),
                  pl.BlockSpec((tm, D_MODEL), lambda i: (i, 0)),
                  _const_spec((kdim, D_MODEL)), _const_spec((1, D_MODEL))],
        out_specs=pl.BlockSpec((tm, D_MODEL), lambda i: (i, 0)),
        out_shape=jax.ShapeDtypeStruct((t, D_MODEL), F32),
        compiler_params=_params(1),
    )(y2, x2, w, gpost)


def _row(v):
    return v.reshape(1, -1).astype(F32)


def _pad_heads(w, width, offset=0):
    k, hh, d = w.shape
    out = jnp.zeros((k, hh, width), w.dtype)
    out = out.at[:, :, offset:offset + d].set(w)
    return out.reshape(k, hh * width)


def _rot_half_cols(w):
    half = w.shape[-1] // 2
    return jnp.concatenate([-w[..., half:], w[..., :half]], axis=-1)


def _mla_layer(x2, pos2, b, s, gpre, gpost, w_in, q_norm, w_uq, kv_norm, w_ukv, w_o):
    o1 = MLA_Q_LORA
    o2 = o1 + MLA_KV_LORA
    w_pe = w_in[:, o2:]
    slab = jnp.zeros((D_MODEL, ATT_PAD), F32).at[:, MLA_NOPE:MLA_NOPE + MLA_ROPE]
    wlat = jnp.concatenate([w_in[:, :o2], slab.set(w_pe), slab.set(_rot_half_cols(w_pe))],
                           axis=1).astype(BF16)
    wq3 = w_uq.reshape(o1, MLA_HEADS, MLA_NOPE + MLA_ROPE)
    wq = _pad_heads(wq3, ATT_PAD).astype(BF16)
    wqr = _pad_heads(_rot_half_cols(wq3[:, :, MLA_NOPE:]), ATT_PAD, MLA_NOPE).astype(BF16)
    wkv3 = w_ukv.reshape(MLA_KV_LORA, MLA_HEADS, MLA_NOPE + MLA_V)
    wk = _pad_heads(wkv3[:, :, :MLA_NOPE], ATT_PAD).astype(BF16)
    wv = wkv3[:, :, MLA_NOPE:].reshape(MLA_KV_LORA, MLA_HEADS * MLA_V).astype(BF16)
    half = MLA_ROPE // 2
    inv = ROPE_BASE ** (-jnp.arange(half, dtype=F32) / half)
    inv_packed = jnp.tile(jnp.concatenate([inv, inv]), ROPE_PACK).reshape(1, ATT_PAD)
    q, k, v = _mla_proj(x2, pos2, _row(gpre), wlat, _row(q_norm), _row(kv_norm), wq, wqr, wk, wv,
                        inv_packed)
    o = _attention(q.reshape(b, s, -1), k.reshape(b, s, -1), v.reshape(b, s, -1))
    return o.reshape(b * s, -1), w_o.astype(BF16), _row(gpost)


def kernel(x, positions, norm_mix_pre, norm_mix_post, norm_ffn_pre, norm_ffn_post, ffn_w_in, ffn_w_out, conv_w_pw1, conv_b_pw1, conv_w_dw, conv_b_dw, conv_ln_g, conv_ln_b, conv_w_pw2, conv_b_pw2, ssm_w_in, ssm_conv_w, ssm_conv_b, ssm_dt_bias, ssm_a_log, ssm_d, ssm_norm_w, ssm_w_out, mla_w_in, mla_q_norm, mla_w_uq, mla_kv_norm, mla_w_ukv, mla_w_o):
    b, s, d = x.shape
    t = b * s
    quarter = MLA_SUB // ROPE_PACK
    pos2 = jnp.repeat(positions.reshape(t // MLA_SUB, ROPE_PACK, quarter).transpose(0, 2, 1),
                      MLA_ROPE, axis=2).reshape(t // ROPE_PACK, ATT_PAD)
    ffn_w1 = ffn_w_in.astype(BF16)
    ffn_w2 = ffn_w_out.astype(BF16)
    conv_w1 = conv_w_pw1.astype(BF16)
    conv_w2 = conv_w_pw2.astype(BF16)
    i_conv = i_ssm = i_mla = 0
    for i in range(DEPTH):
        kind = i % N_MIXERS
        mixer_out = None
        gpre, gpost = _row(norm_mix_pre[i]), _row(norm_mix_post[i])
        if kind == 0:
            jx = i_conv
            x = _conv_layer(
                x.reshape(b, s, d), gpre, gpost, conv_w1, _row(conv_b_pw1[jx]),
                conv_w_dw[jx], _row(conv_b_dw[jx]), _row(conv_ln_g[jx]), _row(conv_ln_b[jx]),
                conv_w2, _row(conv_b_pw2[jx]), jx)
            i_conv += 1
        elif kind == 1:
            jx = i_ssm
            w_in = ssm_w_in[jx].astype(BF16)
            o2 = SSM_D_INNER + SSM_CONV_DIM
            rep = SSM_HEAD_DIM
            pad = LANES - SSM_HEADS
            expand = jnp.repeat(jnp.eye(LANES, SSM_HEADS, dtype=BF16), rep, axis=1)
            x = _ssd_layer(
                x.reshape(b, s, d), gpre, gpost, w_in, jnp.pad(w_in[:, o2:], ((0, 0), (0, pad))),
                ssm_conv_w[jx], _row(ssm_conv_b[jx]), _row(jnp.pad(ssm_dt_bias[jx], (0, pad))),
                expand, _row(jnp.pad(ssm_a_log[jx], (0, pad))),
                _row(jnp.repeat(ssm_a_log[jx], rep)), _row(jnp.repeat(ssm_d[jx], rep)),
                _row(ssm_norm_w[jx]), ssm_w_out[jx].astype(BF16))
            i_ssm += 1
        else:
            jx = i_mla
            mixer_out = _mla_layer(x.reshape(t, d), pos2, b, s, norm_mix_pre[i], norm_mix_post[i],
                                   mla_w_in[jx], mla_q_norm[jx], mla_w_uq[jx], mla_kv_norm[jx],
                                   mla_w_ukv[jx], mla_w_o[jx])
            i_mla += 1
        x = _ffn(x.reshape(t, d), _row(norm_ffn_pre[i]), _row(norm_ffn_post[i]), ffn_w1, ffn_w2, i,
                 mixer_out)
    return x.reshape(b, s, d)
```

```python
import functools
import math

import jax
import jax.numpy as jnp
from jax import lax
from jax.experimental import pallas as pl
from jax.experimental.pallas import tpu as pltpu

F32 = jnp.float32
BF16 = jnp.bfloat16

D_MODEL = 1024
DEPTH = 4
CHUNK = 64
N_MIXERS = 3
EPS = 1e-6
FFN_DIM = 4 * D_MODEL
CONV_KERNEL = 31
SSM_D_INNER = 2 * D_MODEL
SSM_HEAD_DIM = 64
SSM_HEADS = SSM_D_INNER // SSM_HEAD_DIM
SSM_GROUPS = 8
SSM_STATE = 128
SSM_CONV = 4
SSM_CONV_DIM = SSM_D_INNER + 2 * SSM_GROUPS * SSM_STATE
MLA_HEADS = D_MODEL // 64
MLA_NOPE = 64
MLA_ROPE = 32
MLA_V = 64
MLA_Q_LORA = 3 * D_MODEL // 8
MLA_KV_LORA = D_MODEL // 4
ROPE_BASE = 10000.0
LOG2E = math.log2(math.e)

LANES = 128
SUBLANES = 8
VMEM_LIMIT_BYTES = 56 * 1024 * 1024

FFN_TOKENS = 1024
FFN_SUB = 512
FFN_CHUNK = 512
CONV_TOKENS = 1024
CONV_HALO = 32
CONV_ROWS = 32
CONV_LN_ROWS = 128
CONV_LANES = 256
SSD_TOKENS = 512
SSD_CHUNK = 128
SSD_HALO = SUBLANES
SSD_CONV_LANES = 512
SSD_AHEAD = 1
MLA_TOKENS = 512
MLA_SUB = 256
ROPE_PACK = 4
ATT_Q = 256
ATT_PAD = 128
ATT_AHEAD = 2


def _const_spec(shape):
    nd = len(shape)
    return pl.BlockSpec(shape, lambda *_: (0,) * nd, pipeline_mode=pl.Buffered(1))


def _layer_spec(shape, layer):
    nd = len(shape)
    return pl.BlockSpec((None,) + tuple(shape), lambda *_: (layer,) + (0,) * nd,
                        pipeline_mode=pl.Buffered(1))


def _params(n_axes, parallel=True):
    sem = ("parallel" if parallel else "arbitrary",) * n_axes
    return pltpu.CompilerParams(dimension_semantics=sem, vmem_limit_bytes=VMEM_LIMIT_BYTES)


def _rms(x, g):
    return x * lax.rsqrt(jnp.mean(x * x, axis=-1, keepdims=True) + EPS) * g


def _silu(x):
    half = 0.5 * x
    return half + half * jnp.tanh(half)


def _dot(a, b):
    return jnp.dot(a, b, preferred_element_type=F32)


def _dot_nt(a, b):
    return lax.dot_general(a, b, (((1,), (1,)), ((), ())), preferred_element_type=F32)


def _dot_tn(a, b):
    return lax.dot_general(a, b, (((0,), (0,)), ((), ())), preferred_element_type=F32)


def _split3(x):
    hi = x.astype(BF16)
    r1 = x - hi.astype(F32)
    mid = r1.astype(BF16)
    lo = (r1 - mid.astype(F32)).astype(BF16)
    return hi, mid, lo


def _dot_exact01(m01, x):
    hi, mid, lo = _split3(x)
    return (_dot(jnp.concatenate([m01, m01], axis=1), jnp.concatenate([hi, mid], axis=0))
            + _dot(m01, lo))


def _dot_exact01_rhs(x, m01):
    hi, mid, lo = _split3(x)
    return (_dot(jnp.concatenate([hi, mid], axis=1), jnp.concatenate([m01, m01], axis=0))
            + _dot(lo, m01))


def _ffn_body(*refs, mixer_out):
    if mixer_out:
        y_ref, wmix_ref, gmix_ref, x_ref, gpre_ref, gpost_ref, w1_ref, w2_ref, o_ref = refs
    else:
        x_ref, gpre_ref, gpost_ref, w1_ref, w2_ref, o_ref = refs
    for sub in range(FFN_TOKENS // FFN_SUB):
        rows = slice(sub * FFN_SUB, (sub + 1) * FFN_SUB)
        x = x_ref[rows, :]
        if mixer_out:
            x = x + _rms(_dot(y_ref[rows, :], wmix_ref[...]), gmix_ref[...])
        h = _rms(x, gpre_ref[...]).astype(BF16)
        acc = None
        for c in range(FFN_DIM // FFN_CHUNK):
            cols = slice(c * FFN_CHUNK, (c + 1) * FFN_CHUNK)
            a = _dot(h, w1_ref[:, cols])
            a = jnp.square(jnp.maximum(a, 0.0)).astype(BF16)
            p = _dot(a, w2_ref[cols, :])
            acc = p if acc is None else acc + p
        o_ref[rows, :] = x + _rms(acc, gpost_ref[...])


def _ffn(x2, gpre, gpost, w1_all, w2_all, layer, mixer_out=None):
    t = x2.shape[0]
    tok = pl.BlockSpec((FFN_TOKENS, D_MODEL), lambda i: (i, 0))
    vec = _const_spec((1, D_MODEL))
    specs = [tok, vec, vec,
             _layer_spec((D_MODEL, FFN_DIM), layer), _layer_spec((FFN_DIM, D_MODEL), layer)]
    args = (x2, gpre, gpost, w1_all, w2_all)
    if mixer_out is not None:
        y2, w_mix, g_mix = mixer_out
        specs = [pl.BlockSpec((FFN_TOKENS, y2.shape[1]), lambda i: (i, 0)),
                 _const_spec(w_mix.shape), vec] + specs
        args = (y2, w_mix, g_mix) + args
    return pl.pallas_call(
        functools.partial(_ffn_body, mixer_out=mixer_out is not None),
        name="ffn",
        grid=(t // FFN_TOKENS,),
        in_specs=specs,
        out_specs=tok,
        out_shape=jax.ShapeDtypeStruct((t, D_MODEL), F32),
        compiler_params=_params(1),
    )(*args)


def _conv_body(x_ref, gpre_ref, gpost_ref, w1_ref, b1_ref, wdw_ref, bdw_ref,
               lng_ref, lnb_ref, w2_ref, b2_ref, o_ref, ubuf, shifted_ref, dbuf):
    ts = x_ref.shape[0]
    j = pl.program_id(1)

    @pl.when(j == 0)
    def _():
        ubuf[0:CONV_HALO, :] = jnp.zeros((CONV_HALO, D_MODEL), F32)

    @pl.when(j > 0)
    def _():
        ubuf[0:CONV_HALO, :] = ubuf[ts:ts + CONV_HALO, :]

    x = x_ref[...]
    h = _rms(x, gpre_ref[...]).astype(BF16)
    first = CONV_HALO - (CONV_KERNEL - 1)
    n_sh = CONV_HALO + ts - SUBLANES

    for cblk in range(D_MODEL // CONV_LANES):
        shifted = shifted_ref.at[cblk % 2]
        cs = slice(cblk * CONV_LANES, (cblk + 1) * CONV_LANES)
        gs = slice(D_MODEL + cblk * CONV_LANES, D_MODEL + (cblk + 1) * CONV_LANES)
        ua = _dot(h, w1_ref[:, cs]) + b1_ref[:, cs]
        ub = _dot(h, w1_ref[:, gs]) + b1_ref[:, gs]
        ua_half = 0.5 * ua
        ubuf[CONV_HALO:CONV_HALO + ts, cs] = ua_half + ua_half * jnp.tanh(0.5 * ub)
        window = ubuf[:, cs]
        for sft in range(1, SUBLANES):
            rolled = pltpu.roll(window, CONV_HALO + ts - sft, axis=0)
            shifted[sft - 1, 0:n_sh, :] = rolled[0:n_sh, :]
        for sub in range(CONV_LANES // LANES):
            ls = slice(cblk * CONV_LANES + sub * LANES, cblk * CONV_LANES + (sub + 1) * LANES)
            bl = slice(sub * LANES, (sub + 1) * LANES)
            for rb in range(ts // CONV_ROWS):
                acc = jnp.broadcast_to(bdw_ref[:, ls], (CONV_ROWS, LANES))
                for k in range(CONV_KERNEL):
                    lo = rb * CONV_ROWS + first + k
                    sft = lo % SUBLANES
                    if sft:
                        tap = shifted[sft - 1, lo - sft:lo - sft + CONV_ROWS, bl]
                    else:
                        tap = ubuf[lo:lo + CONV_ROWS, ls]
                    acc = acc + wdw_ref[k:k + 1, ls] * tap
                dbuf[rb * CONV_ROWS:(rb + 1) * CONV_ROWS, ls] = acc

    for rt in range(ts // CONV_LN_ROWS):
        rows = slice(rt * CONV_LN_ROWS, (rt + 1) * CONV_LN_ROWS)
        acc = dbuf[rows, :]
        mu = jnp.mean(acc, axis=-1, keepdims=True)
        cen = acc - mu
        var = jnp.mean(cen * cen, axis=-1, keepdims=True)
        y = cen * lax.rsqrt(var + EPS) * lng_ref[...] + lnb_ref[...]
        v = _silu(y).astype(BF16)
        out = _dot(v, w2_ref[...]) + b2_ref[...]
        o_ref[rows, :] = x[rows, :] + _rms(out, gpost_ref[...])


def _conv_layer(x3, gpre, gpost, w1_all, b1, wdw, bdw, lng, lnb, w2_all, b2, layer):
    b, s, _ = x3.shape
    ts = CONV_TOKENS
    tok = pl.BlockSpec((None, ts, D_MODEL), lambda bi, j: (bi, j, 0))
    vec = _const_spec((1, D_MODEL))
    return pl.pallas_call(
        _conv_body,
        name="conv_mixer",
        grid=(b, s // ts),
        in_specs=[tok, vec, vec, _layer_spec((D_MODEL, 2 * D_MODEL), layer),
                  _const_spec((1, 2 * D_MODEL)), _const_spec((CONV_KERNEL, D_MODEL)), vec,
                  vec, vec, _layer_spec((D_MODEL, D_MODEL), layer), vec],
        out_specs=tok,
        out_shape=jax.ShapeDtypeStruct(x3.shape, F32),
        scratch_shapes=[pltpu.VMEM((CONV_HALO + ts, D_MODEL), F32),
                        pltpu.VMEM((2, SUBLANES - 1, CONV_HALO + ts, CONV_LANES), F32),
                        pltpu.VMEM((ts, D_MODEL), F32)],
        compiler_params=_params(2, parallel=False),
    )(x3, gpre, gpost, w1_all, b1, wdw, bdw, lng, lnb, w2_all, b2)


def _ssd_body(x_ref, gpre_ref, gpost_ref, win_ref, wdt_ref, cw_ref, cb_ref, dtb_ref,
              expand_ref, alogh_ref, alog_ref, dskip_ref, nw_ref, wout_ref, o_ref, hbuf, xbuf,
              dtbuf, zbuf, ybuf, state):
    ts = x_ref.shape[0]
    lc = SSD_CHUNK
    gw = SSM_D_INNER // SSM_GROUPS
    j = pl.program_id(1)

    @pl.when(j == 0)
    def _():
        hbuf[...] = jnp.zeros(hbuf.shape, F32)
        state[...] = jnp.zeros(state.shape, F32)

    x = x_ref[...]
    h = _rms(x, gpre_ref[...]).astype(BF16)
    dt_raw = _dot(h, wdt_ref[...]) + dtb_ref[...]
    dt_heads = jnp.maximum(dt_raw, 0.0) + jnp.log1p(jnp.exp(-jnp.abs(dt_raw)))
    dtbuf[...] = _dot_exact01_rhs(dt_heads, expand_ref[...])

    n_cblk = SSM_CONV_DIM // SSD_CONV_LANES

    def project(cblk):
        ws = slice(SSM_D_INNER + cblk * SSD_CONV_LANES, SSM_D_INNER + (cblk + 1) * SSD_CONV_LANES)
        return _dot(h, win_ref[:, ws])

    raw = project(0)
    for cblk in range(n_cblk):
        cs = slice(cblk * SSD_CONV_LANES, (cblk + 1) * SSD_CONV_LANES)
        if cblk + 1 < n_cblk:
            raw_next = project(cblk + 1)
        else:
            raw_next = None
            zbuf[...] = _dot(h, win_ref[:, 0:SSM_D_INNER])
        window = jnp.concatenate([hbuf[:, cs], raw], axis=0)
        hbuf[:, cs] = raw[ts - SSD_HALO:ts, :]
        raw = raw_next
        conv = jnp.broadcast_to(cb_ref[:, cs], (ts, SSD_CONV_LANES))
        for k in range(SSM_CONV):
            back = SSM_CONV - 1 - k
            tap = pltpu.roll(window, back, axis=0) if back else window
            conv = conv + cw_ref[k:k + 1, cs] * tap[SSD_HALO:SSD_HALO + ts, :]
        xbuf[:, cs] = _silu(conv)

    a_row = -jnp.exp(alog_ref[...]) * LOG2E
    a_heads = -jnp.exp(alogh_ref[...]) * LOG2E
    ri = lax.broadcasted_iota(jnp.int32, (lc, lc), 0)
    ci = lax.broadcasted_iota(jnp.int32, (lc, lc), 1)
    causal = ci <= ri
    tril01 = causal.astype(BF16)
    off_b = SSM_D_INNER
    off_c = SSM_D_INNER + SSM_GROUPS * SSM_STATE
    lane = lax.broadcasted_iota(jnp.int32, (1, gw), 1)
    head01 = [((lane >= r * SSM_HEAD_DIM) & (lane < (r + 1) * SSM_HEAD_DIM)).astype(BF16)
              for r in range(gw // SSM_HEAD_DIM)]

    def chunk_terms(c):
        rows = slice(c * lc, (c + 1) * lc)
        dt = dtbuf[rows, :]
        a_cs = _dot_exact01(tril01, dt * a_row)
        a_last = a_cs[lc - 1:lc, :]
        a_csh = _dot_exact01(tril01, dt_heads[rows, :] * a_heads)
        return dict(rows=rows, dt=dt, exp_a=jnp.exp2(a_cs), to_end=jnp.exp2(a_last - a_cs),
                    decay_chunk=jnp.exp2(a_last), a_csh=a_csh, a_csh_t=a_csh.T)

    def group_inputs(ck, g):
        rows = ck["rows"]
        gl = slice(g * gw, (g + 1) * gw)
        xs = xbuf[rows, gl]
        xdt = xs * ck["dt"][:, gl]
        bm = xbuf[rows, off_b + g * SSM_STATE:off_b + (g + 1) * SSM_STATE].astype(BF16)
        cm = xbuf[rows, off_c + g * SSM_STATE:off_c + (g + 1) * SSM_STATE].astype(BF16)
        return xs, xdt, bm, cm, _dot_nt(cm, bm)

    def group_outputs(ck, g, xs, xdt, bm, cm, cb):
        rows = ck["rows"]
        gl = slice(g * gw, (g + 1) * gw)
        xdb = xdt.astype(BF16)
        ms, xds = [], []
        for r in range(gw // SSM_HEAD_DIM):
            hd = g * (gw // SSM_HEAD_DIM) + r
            col = ck["a_csh"][:, hd:hd + 1]
            row = ck["a_csh_t"][hd:hd + 1, :]
            decay = jnp.where(causal, jnp.exp2(col - row), 0.0)
            ms.append((cb * decay).astype(BF16))
            xds.append(xdb * head01[r])
        y = (_dot(jnp.concatenate(ms[0:2], axis=1), jnp.concatenate(xds[0:2], axis=0))
             + _dot(jnp.concatenate(ms[2:4], axis=1), jnp.concatenate(xds[2:4], axis=0)))
        st = state[g]
        y = y + _dot(cm, st.astype(BF16)) * ck["exp_a"][:, gl]
        xw = (xdt * ck["to_end"][:, gl]).astype(BF16)
        state[g] = st * ck["decay_chunk"][:, gl] + _dot_tn(bm, xw)
        y = y + dskip_ref[:, gl] * xs
        y = y * _silu(zbuf[rows, gl])
        y = y * lax.rsqrt(jnp.mean(y * y, axis=-1, keepdims=True) + EPS) * nw_ref[:, gl]
        ybuf[rows, gl] = y.astype(BF16)

    units = [(c, g) for c in range(ts // lc) for g in range(SSM_GROUPS)]
    chunks = {}

    def issue(n):
        c1, g1 = units[n]
        if c1 not in chunks:
            chunks[c1] = chunk_terms(c1)
        return group_inputs(chunks[c1], g1)

    pending = [issue(n) for n in range(SSD_AHEAD)]
    for n, (c, g) in enumerate(units):
        current = pending.pop(0)
        if n + SSD_AHEAD < len(units):
            pending.append(issue(n + SSD_AHEAD))
        group_outputs(chunks[c], g, *current)
        if g == SSM_GROUPS - 1:
            rows = chunks[c]["rows"]
            out = _dot(ybuf[rows, :], wout_ref[...])
            o_ref[rows, :] = x[rows, :] + _rms(out, gpost_ref[...])


def _ssd_layer(x3, gpre, gpost, win, wdt, cw, cb, dtb, expand, alogh, alog, dskip, nw, wout):
    b, s, _ = x3.shape
    ts = SSD_TOKENS
    tok = pl.BlockSpec((None, ts, D_MODEL), lambda bi, j: (bi, j, 0))
    vec = _const_spec((1, D_MODEL))
    inner = _const_spec((1, SSM_D_INNER))
    return pl.pallas_call(
        _ssd_body,
        name="ssd_mixer",
        grid=(b, s // ts),
        in_specs=[tok, vec, vec, _const_spec(win.shape), _const_spec((D_MODEL, LANES)),
                  _const_spec((SSM_CONV, SSM_CONV_DIM)), _const_spec((1, SSM_CONV_DIM)),
                  _const_spec((1, LANES)), _const_spec((LANES, SSM_D_INNER)), _const_spec((1, LANES)),
                  inner, inner, inner, _const_spec((SSM_D_INNER, D_MODEL))],
        out_specs=tok,
        out_shape=jax.ShapeDtypeStruct(x3.shape, F32),
        scratch_shapes=[pltpu.VMEM((SSD_HALO, SSM_CONV_DIM), F32),
                        pltpu.VMEM((ts, SSM_CONV_DIM), F32),
                        pltpu.VMEM((ts, SSM_D_INNER), F32),
                        pltpu.VMEM((ts, SSM_D_INNER), F32),
                        pltpu.VMEM((ts, SSM_D_INNER), BF16),
                        pltpu.VMEM((SSM_GROUPS, SSM_STATE, SSM_D_INNER // SSM_GROUPS), F32)],
        compiler_params=_params(2, parallel=False),
    )(x3, gpre, gpost, win, wdt, cw, cb, dtb, expand, alogh, alog, dskip, nw, wout)


def _mla_proj_body(x_ref, pos_ref, gpre_ref, wlat_ref, qn_ref, kvn_ref, wq_ref, wqr_ref, wk_ref,
                   wv_ref, inv_ref, q_ref, k_ref, v_ref):
    o1 = MLA_Q_LORA
    o2 = o1 + MLA_KV_LORA
    reps = (1, MLA_HEADS)

    def latent(sub):
        rows = slice(sub * MLA_SUB, (sub + 1) * MLA_SUB)
        h = _rms(x_ref[rows, :], gpre_ref[...]).astype(BF16)
        return _dot(h, wlat_ref[...])

    packed = MLA_SUB // ROPE_PACK
    rope_lane = lax.broadcasted_iota(jnp.int32, (packed, ATT_PAD), 1)
    rope_lane = (rope_lane >= MLA_NOPE) & (rope_lane < MLA_NOPE + MLA_ROPE)

    def rope_terms(sub):
        ang = pos_ref[sub * packed:(sub + 1) * packed, :].astype(F32) * inv_ref[...]
        cos_p, sin_p = jnp.cos(ang), jnp.sin(ang)
        cos, sin = [], []
        for q in range(ROPE_PACK):
            shift = (MLA_NOPE - MLA_ROPE * q) % ATT_PAD
            cos_q = pltpu.roll(cos_p, shift, axis=1) if shift else cos_p
            sin_q = pltpu.roll(sin_p, shift, axis=1) if shift else sin_p
            cos.append(jnp.where(rope_lane, cos_q, 1.0))
            sin.append(jnp.where(rope_lane, sin_q, 0.0))
        return jnp.concatenate(cos, axis=0), jnp.concatenate(sin, axis=0)

    n_sub = x_ref.shape[0] // MLA_SUB
    lat_next = latent(0)
    rope_next = rope_terms(0)
    for sub in range(n_sub):
        rows = slice(sub * MLA_SUB, (sub + 1) * MLA_SUB)
        lat, (cos, sin) = lat_next, rope_next
        if sub + 1 < n_sub:
            lat_next = latent(sub + 1)
            rope_next = rope_terms(sub + 1)
        cq = _rms(lat[:, :o1], qn_ref[...]).astype(BF16)
        ckv = _rms(lat[:, o1:o2], kvn_ref[...]).astype(BF16)
        pe = lat[:, o2:o2 + ATT_PAD]
        pe_rot = lat[:, o2 + ATT_PAD:o2 + 2 * ATT_PAD]
        k_pe = pe * cos + pe_rot * sin
        k_ref[rows, :] = (_dot(ckv, wk_ref[...]) + jnp.tile(k_pe, reps)).astype(BF16)
        q = (_dot(cq, wq_ref[...]) * jnp.tile(cos, reps)
             + _dot(cq, wqr_ref[...]) * jnp.tile(sin, reps))
        q_ref[rows, :] = q.astype(BF16)
        v_ref[rows, :] = _dot(ckv, wv_ref[...]).astype(BF16)


def _mla_proj(x2, pos2, gpre, wlat, qn, kvn, wq, wqr, wk, wv, inv):
    t = x2.shape[0]
    tm = MLA_TOKENS
    hp = MLA_HEADS * ATT_PAD
    hv = MLA_HEADS * MLA_V

    def tok(w):
        return pl.BlockSpec((tm, w), lambda i: (i, 0))

    return pl.pallas_call(
        _mla_proj_body,
        name="mla_proj",
        grid=(t // tm,),
        in_specs=[tok(D_MODEL), pl.BlockSpec((tm // ROPE_PACK, ATT_PAD), lambda i: (i, 0)),
                  _const_spec((1, D_MODEL)), _const_spec(wlat.shape),
                  _const_spec((1, MLA_Q_LORA)), _const_spec((1, MLA_KV_LORA)),
                  _const_spec((MLA_Q_LORA, hp)), _const_spec((MLA_Q_LORA, hp)),
                  _const_spec((MLA_KV_LORA, hp)), _const_spec((MLA_KV_LORA, hv)),
                  _const_spec((1, ATT_PAD))],
        out_specs=[tok(hp), tok(hp), tok(hv)],
        out_shape=[jax.ShapeDtypeStruct((t, hp), BF16), jax.ShapeDtypeStruct((t, hp), BF16),
                   jax.ShapeDtypeStruct((t, hv), BF16)],
        compiler_params=_params(1),
    )(x2, pos2, gpre, wlat, qn, kvn, wq, wqr, wk, wv, inv)


def _attn_body(q_ref, k_ref, v_ref, o_ref, s_buf):
    s_len = q_ref.shape[0]
    tq = ATT_Q
    c2 = (MLA_NOPE + MLA_ROPE) ** -0.5 * math.log2(math.e)
    ri = lax.broadcasted_iota(jnp.int32, (tq, tq), 0) // CHUNK
    ci = lax.broadcasted_iota(jnp.int32, (tq, tq), 1) // CHUNK
    diag_mask = ci <= ri
    lane = lax.broadcasted_iota(jnp.int32, (tq, 2 * MLA_V), 1)

    def scores(i, e):
        lo = i * tq
        qrows = slice(lo, lo + tq)
        cols = slice(e * ATT_PAD, (e + 1) * ATT_PAD)
        sb = s_buf.at[2 * (i % 2) + e]
        q = q_ref[qrows, cols]
        sb[:, qrows] = jnp.where(diag_mask, _dot_nt(q, k_ref[qrows, cols]), -jnp.inf)
        if lo:
            sb[:, 0:lo] = _dot_nt(q, k_ref[0:lo, cols])
        return jnp.max(sb[:, 0:lo + tq], axis=-1, keepdims=True)

    def weighted_values(i, e, m):
        hi = (i + 1) * tq
        sb = s_buf.at[2 * (i % 2) + e]
        p = jnp.exp2((sb[:, 0:hi] - m) * c2)
        l = jnp.sum(p, axis=-1, keepdims=True)
        return _dot(p.astype(BF16), v_ref[0:hi, :]) / l

    units = [(i, e) for i in reversed(range(s_len // tq)) for e in range(2)]
    pending = [scores(*u) for u in units[:ATT_AHEAD]]
    outs = {}
    for n, (i, e) in enumerate(units):
        m = pending.pop(0)
        if n + ATT_AHEAD < len(units):
            pending.append(scores(*units[n + ATT_AHEAD]))
        outs[e] = weighted_values(i, e, m)
        if e == 1:
            qrows = slice(i * tq, (i + 1) * tq)
            o_ref[qrows, :] = jnp.where(lane < MLA_V, outs[0], outs[1]).astype(BF16)


def _attention(q3, k3, v3):
    b, s, _ = q3.shape
    return pl.pallas_call(
        _attn_body,
        name="mla_attn",
        grid=(b, MLA_HEADS // 2),
        in_specs=[pl.BlockSpec((None, s, 2 * ATT_PAD), lambda bi, hp: (bi, 0, hp)),
                  pl.BlockSpec((None, s, 2 * ATT_PAD), lambda bi, hp: (bi, 0, hp)),
                  pl.BlockSpec((None, s, 2 * MLA_V), lambda bi, hp: (bi, 0, hp))],
        out_specs=pl.BlockSpec((None, s, 2 * MLA_V), lambda bi, hp: (bi, 0, hp)),
        out_shape=jax.ShapeDtypeStruct((b, s, MLA_HEADS * MLA_V), BF16),
        scratch_shapes=[pltpu.VMEM((4, ATT_Q, s), F32)],
        compiler_params=_params(2),
    )(q3, k3, v3)


def _row(v):
    return v.reshape(1, -1).astype(F32)


def _pad_heads(w, width, offset=0):
    k, hh, d = w.shape
    out = jnp.zeros((k, hh, width), w.dtype)
    out = out.at[:, :, offset:offset + d].set(w)
    return out.reshape(k, hh * width)


def _rot_half_cols(w):
    half = w.shape[-1] // 2
    return jnp.concatenate([-w[..., half:], w[..., :half]], axis=-1)


def _mla_layer(x2, pos2, b, s, gpre, gpost, w_in, q_norm, w_uq, kv_norm, w_ukv, w_o):
    o1 = MLA_Q_LORA
    o2 = o1 + MLA_KV_LORA
    w_pe = w_in[:, o2:]
    slab = jnp.zeros((D_MODEL, ATT_PAD), F32).at[:, MLA_NOPE:MLA_NOPE + MLA_ROPE]
    wlat = jnp.concatenate([w_in[:, :o2], slab.set(w_pe), slab.set(_rot_half_cols(w_pe))],
                           axis=1).astype(BF16)
    wq3 = w_uq.reshape(o1, MLA_HEADS, MLA_NOPE + MLA_ROPE)
    wq = _pad_heads(wq3, ATT_PAD).astype(BF16)
    wqr = _pad_heads(_rot_half_cols(wq3[:, :, MLA_NOPE:]), ATT_PAD, MLA_NOPE).astype(BF16)
    wkv3 = w_ukv.reshape(MLA_KV_LORA, MLA_HEADS, MLA_NOPE + MLA_V)
    wk = _pad_heads(wkv3[:, :, :MLA_NOPE], ATT_PAD).astype(BF16)
    wv = wkv3[:, :, MLA_NOPE:].reshape(MLA_KV_LORA, MLA_HEADS * MLA_V).astype(BF16)
    half = MLA_ROPE // 2
    inv = ROPE_BASE ** (-jnp.arange(half, dtype=F32) / half)
    inv_packed = jnp.tile(jnp.concatenate([inv, inv]), ROPE_PACK).reshape(1, ATT_PAD)
    q, k, v = _mla_proj(x2, pos2, _row(gpre), wlat, _row(q_norm), _row(kv_norm), wq, wqr, wk, wv,
                        inv_packed)
    o = _attention(q.reshape(b, s, -1), k.reshape(b, s, -1), v.reshape(b, s, -1))
    return o.reshape(b * s, -1), w_o.astype(BF16), _row(gpost)


def kernel(x, positions, norm_mix_pre, norm_mix_post, norm_ffn_pre, norm_ffn_post, ffn_w_in, ffn_w_out, conv_w_pw1, conv_b_pw1, conv_w_dw, conv_b_dw, conv_ln_g, conv_ln_b, conv_w_pw2, conv_b_pw2, ssm_w_in, ssm_conv_w, ssm_conv_b, ssm_dt_bias, ssm_a_log, ssm_d, ssm_norm_w, ssm_w_out, mla_w_in, mla_q_norm, mla_w_uq, mla_kv_norm, mla_w_ukv, mla_w_o):
    b, s, d = x.shape
    t = b * s
    quarter = MLA_SUB // ROPE_PACK
    pos2 = jnp.repeat(positions.reshape(t // MLA_SUB, ROPE_PACK, quarter).transpose(0, 2, 1),
                      MLA_ROPE, axis=2).reshape(t // ROPE_PACK, ATT_PAD)
    ffn_w1 = ffn_w_in.astype(BF16)
    ffn_w2 = ffn_w_out.astype(BF16)
    conv_w1 = conv_w_pw1.astype(BF16)
    conv_w2 = conv_w_pw2.astype(BF16)
    i_conv = i_ssm = i_mla = 0
    for i in range(DEPTH):
        kind = i % N_MIXERS
        mixer_out = None
        gpre, gpost = _row(norm_mix_pre[i]), _row(norm_mix_post[i])
        if kind == 0:
            jx = i_conv
            x = _conv_layer(
                x.reshape(b, s, d), gpre, gpost, conv_w1, _row(conv_b_pw1[jx]),
                conv_w_dw[jx], _row(conv_b_dw[jx]), _row(conv_ln_g[jx]), _row(conv_ln_b[jx]),
                conv_w2, _row(conv_b_pw2[jx]), jx)
            i_conv += 1
        elif kind == 1:
            jx = i_ssm
            w_in = ssm_w_in[jx].astype(BF16)
            o2 = SSM_D_INNER + SSM_CONV_DIM
            rep = SSM_HEAD_DIM
            pad = LANES - SSM_HEADS
            expand = jnp.repeat(jnp.eye(LANES, SSM_HEADS, dtype=BF16), rep, axis=1)
            x = _ssd_layer(
                x.reshape(b, s, d), gpre, gpost, w_in, jnp.pad(w_in[:, o2:], ((0, 0), (0, pad))),
                ssm_conv_w[jx], _row(ssm_conv_b[jx]), _row(jnp.pad(ssm_dt_bias[jx], (0, pad))),
                expand, _row(jnp.pad(ssm_a_log[jx], (0, pad))),
                _row(jnp.repeat(ssm_a_log[jx], rep)), _row(jnp.repeat(ssm_d[jx], rep)),
                _row(ssm_norm_w[jx]), ssm_w_out[jx].astype(BF16))
            i_ssm += 1
        else:
            jx = i_mla
            mixer_out = _mla_layer(x.reshape(t, d), pos2, b, s, norm_mix_pre[i], norm_mix_post[i],
                                   mla_w_in[jx], mla_q_norm[jx], mla_w_uq[jx], mla_kv_norm[jx],
                                   mla_w_ukv[jx], mla_w_o[jx])
            i_mla += 1
        x = _ffn(x.reshape(t, d), _row(norm_ffn_pre[i]), _row(norm_ffn_post[i]), ffn_w1, ffn_w2, i,
                 mixer_out)
    return x.reshape(b, s, d)
```
